```python
import jax, jax.numpy as jnp
from jax import lax
import numpy as np


D_MODEL = 1024
BATCH = 8
SEQ = 2048
DEPTH = 2

GRID_W = 64
HEAD_DIM = 64
N_HEADS_TOTAL = D_MODEL // HEAD_DIM
A_HEADS = N_HEADS_TOTAL // 4
B_Q_HEADS = N_HEADS_TOTAL // 2
B_KV_HEADS = B_Q_HEADS // 4
C_HEADS = N_HEADS_TOTAL // 4
A_WIDTH = A_HEADS * HEAD_DIM
B_Q_WIDTH = B_Q_HEADS * HEAD_DIM
B_KV_WIDTH = B_KV_HEADS * HEAD_DIM
C_WIDTH = C_HEADS * HEAD_DIM
N_BRANCH = 3
IN_SPLIT_SIZES = (A_WIDTH, A_WIDTH, A_WIDTH, B_Q_WIDTH, B_KV_WIDTH, B_KV_WIDTH,
                  C_WIDTH, C_WIDTH, C_WIDTH, N_BRANCH * D_MODEL)
IN_WIDTH = 3 * A_WIDTH + B_Q_WIDTH + 2 * B_KV_WIDTH + 3 * C_WIDTH + N_BRANCH * D_MODEL
A_PATTERNS = ((128, 1), (512, 4), (2048, 16))
C_WIN_ROWS = 8
C_WIN_COLS = 16
Q_BLOCK = 128
ROPE_THETA = 10000.0
D_FF = 3 * D_MODEL
CONV_WIDTH = 3
EPS = 1e-6
NEG_INF = -1e30

kernel_name = 'hybrid_gated_dilated_axial_neighborhood_encoder'


def rms_norm(x, g):
    xf = x.astype(jnp.float32)
    y = xf * lax.rsqrt(jnp.mean(xf * xf, axis=-1, keepdims=True) + EPS)
    return (y * g.astype(jnp.float32)).astype(x.dtype)


def rope_angles(pos, dim):
    inv = ROPE_THETA ** (-jnp.arange(0, dim, 2, dtype=jnp.float32) / dim)
    return pos[:, None] * inv[None, :]


def apply_rope(x, ang):
    d2 = x.shape[-1] // 2
    cos = jnp.cos(ang)[None, :, None, :]
    sin = jnp.sin(ang)[None, :, None, :]
    xf = x.astype(jnp.float32)
    x1, x2 = xf[..., :d2], xf[..., d2:]
    return jnp.concatenate([x1 * cos - x2 * sin, x1 * sin + x2 * cos], axis=-1).astype(x.dtype)


def _split_last(x, sizes):
    out, start = [], 0
    for n in sizes:
        out.append(x[..., start:start + n])
        start += n
    return out


def _dilated_band_attention(q, k, v, dil, radius):
    B, S, H, d = q.shape
    L = S // dil
    N = B * dil

    def to_sub(t):
        return t.reshape(B, L, dil, H, d).transpose(0, 2, 1, 3, 4).reshape(N, L, H, d)

    def from_sub(t):
        tail = t.shape[2:]
        t = t.reshape((B, dil, L) + tail)
        t = t.transpose((0, 2, 1) + tuple(range(3, 3 + len(tail))))
        return t.reshape((B, S) + tail)

    qs, ks, vs = to_sub(q), to_sub(k), to_sub(v)
    qb_len = min(Q_BLOCK, L)
    nblk = -(-L // qb_len)
    Lp = nblk * qb_len
    kb_len = qb_len + 2 * radius
    qs = jnp.pad(qs, ((0, 0), (0, Lp - L), (0, 0), (0, 0))).reshape(N, nblk, qb_len, H, d)
    pad_kv = ((0, 0), (radius, Lp - L + radius), (0, 0), (0, 0))
    kidx = jnp.arange(nblk)[:, None] * qb_len + jnp.arange(kb_len)[None, :]
    kg = jnp.pad(ks, pad_kv)[:, kidx]
    vg = jnp.pad(vs, pad_kv)[:, kidx]
    s = jnp.einsum('nbqhd,nbkhd->nbhqk', qs, kg, preferred_element_type=jnp.float32) * (d ** -0.5)
    key_pos = kidx - radius
    q_pos = jnp.arange(nblk)[:, None] * qb_len + jnp.arange(qb_len)[None, :]
    rel = key_pos[:, None, :] - q_pos[:, :, None]
    valid = (jnp.abs(rel) <= radius) & (key_pos[:, None, :] >= 0) & (key_pos[:, None, :] < L)
    s = jnp.where(valid[None, :, None, :, :], s, NEG_INF)
    m = jnp.max(s, axis=-1, keepdims=True)
    p = jnp.exp(s - m)
    den = jnp.sum(p, axis=-1, keepdims=True)
    lse = (m + jnp.log(den))[..., 0]
    o = jnp.einsum('nbhqk,nbkhd->nbqhd', p / den, vg)
    o = o.reshape(N, Lp, H, d)[:, :L]
    lse = lse.transpose(0, 1, 3, 2).reshape(N, Lp, H)[:, :L]
    return from_sub(o), from_sub(lse)


def dilated_mixture_attention(q, k, v):
    outs, lses = [], []
    for window, dil in A_PATTERNS:
        o, lse = _dilated_band_attention(q, k, v, dil, window // (2 * dil))
        outs.append(o)
        lses.append(lse)
    wts = jax.nn.softmax(jnp.stack(lses, axis=0), axis=0)
    out = jnp.sum(wts[..., None] * jnp.stack(outs, axis=0), axis=0)
    return out.astype(q.dtype)


def gqa_block_attention(q, k, v):
    B, S, Hq, d = q.shape
    Hkv = k.shape[2]
    G = Hq // Hkv
    nblk = S // Q_BLOCK
    qb = q.reshape(B, nblk, Q_BLOCK, Hkv, G, d).transpose(1, 0, 2, 3, 4, 5)

    def one_block(qi):
        s = jnp.einsum('bqhgd,bkhd->bhgqk', qi, k, preferred_element_type=jnp.float32) * (d ** -0.5)
        p = jax.nn.softmax(s, axis=-1)
        return jnp.einsum('bhgqk,bkhd->bqhgd', p, v).astype(q.dtype)

    o = lax.map(one_block, qb)
    return o.transpose(1, 0, 2, 3, 4, 5).reshape(B, S, Hq, d)


def neighborhood_attention_2d(q, k, v, rpb):
    B, S, H, d = q.shape
    R = S // GRID_W
    wr = min(C_WIN_ROWS, R)
    ncb = GRID_W // C_WIN_COLS
    kc_len = 2 * C_WIN_COLS
    rows = jnp.arange(R)
    cols = jnp.arange(GRID_W)
    key_rows = jnp.clip(rows - wr // 2, 0, R - wr)[:, None] + jnp.arange(wr)[None, :]
    col_start = jnp.clip(cols - C_WIN_COLS // 2, 0, GRID_W - C_WIN_COLS)
    blk_start = jnp.clip(jnp.arange(ncb) * C_WIN_COLS - C_WIN_COLS // 2, 0, GRID_W - kc_len)
    key_cols = blk_start[:, None] + jnp.arange(kc_len)[None, :]
    q_cols = cols.reshape(ncb, C_WIN_COLS)
    qg = q.reshape(B, R, ncb, C_WIN_COLS, H, d)
    kgrid = k.reshape(B, R, GRID_W, H, d)
    vgrid = v.reshape(B, R, GRID_W, H, d)
    ridx = key_rows[:, :, None, None]
    cidx = key_cols[None, None, :, :]
    kg = kgrid[:, ridx, cidx]
    vg = vgrid[:, ridx, cidx]
    s = jnp.einsum('brcqhd,bricjhd->bhrcqij', qg, kg, preferred_element_type=jnp.float32) * (d ** -0.5)
    kcol = key_cols[:, None, :]
    qstart = col_start[q_cols][:, :, None]
    col_ok = (kcol >= qstart) & (kcol < qstart + C_WIN_COLS)
    dc_idx = jnp.clip(kcol - q_cols[:, :, None], -(C_WIN_COLS - 1), C_WIN_COLS - 1) + (C_WIN_COLS - 1)
    dr_idx = key_rows - rows[:, None] + (C_WIN_ROWS - 1)
    bias = rpb[:, dr_idx[:, None, None, :, None], dc_idx[None, :, :, None, :]]
    s = s + bias[None].astype(jnp.float32)
    s = jnp.where(col_ok[None, None, None, :, :, None, :], s, NEG_INF)
    p = jax.nn.softmax(s.reshape(s.shape[:5] + (wr * kc_len,)), axis=-1).reshape(s.shape)
    o = jnp.einsum('bhrcqij,bricjhd->brcqhd', p, vg)
    return o.reshape(B, S, H, d).astype(q.dtype)


def centred_depthwise_conv(u, w, b):
    half = CONV_WIDTH // 2
    S = u.shape[1]
    up = jnp.pad(u, ((0, 0), (half, half), (0, 0)))
    out = b + up[:, 0:S] * w[0]
    for j in range(1, CONV_WIDTH):
        out = out + up[:, j:j + S] * w[j]
    return out


def hybrid_layer(x, w_in, b_gate, qk_gain, rpb, w_branch, w_out, norm_mix, norm_ffn,
                 w_up, conv_w, conv_b, w_down, ang_1d, ang_2d):
    B, S, D = x.shape
    h = rms_norm(x, norm_mix)
    qa, ka, va, qb, kb, vb, qc, kc, vc, gate_logits = _split_last(h @ w_in, IN_SPLIT_SIZES)

    def heads(t, n):
        return t.reshape(B, S, n, HEAD_DIM)

    qa = apply_rope(rms_norm(heads(qa, A_HEADS), qk_gain[0, 0]), ang_1d)
    ka = apply_rope(rms_norm(heads(ka, A_HEADS), qk_gain[0, 1]), ang_1d)
    oa = dilated_mixture_attention(qa, ka, heads(va, A_HEADS)).reshape(B, S, A_WIDTH)
    qb = apply_rope(rms_norm(heads(qb, B_Q_HEADS), qk_gain[1, 0]), ang_2d)
    kb = apply_rope(rms_norm(heads(kb, B_KV_HEADS), qk_gain[1, 1]), ang_2d)
    ob = gqa_block_attention(qb, kb, heads(vb, B_KV_HEADS)).reshape(B, S, B_Q_WIDTH)
    qc = rms_norm(heads(qc, C_HEADS), qk_gain[2, 0])
    kc = rms_norm(heads(kc, C_HEADS), qk_gain[2, 1])
    oc = neighborhood_attention_2d(qc, kc, heads(vc, C_HEADS), rpb).reshape(B, S, C_WIDTH)
    ya = oa @ w_branch[:A_WIDTH]
    yb = ob @ w_branch[A_WIDTH:A_WIDTH + B_Q_WIDTH]
    yc = oc @ w_branch[A_WIDTH + B_Q_WIDTH:]
    gates = jax.nn.sigmoid(gate_logits + b_gate).reshape(B, S, N_BRANCH, D)
    merged = gates[:, :, 0] * ya + gates[:, :, 1] * yb + gates[:, :, 2] * yc
    x = x + merged @ w_out
    h = rms_norm(x, norm_ffn)
    u = centred_depthwise_conv(h @ w_up, conv_w, conv_b)
    gate, val = u[..., :D_FF], u[..., D_FF:]
    return x + (jax.nn.gelu(gate, approximate=True) * val) @ w_down


def setup_inputs(seed: int = 0) -> dict:
    key = jax.random.key(seed)
    ks = jax.random.split(key, 16)
    f32 = jnp.float32

    def nrm(k, shape, scale):
        return jax.random.normal(k, shape, f32) * scale

    kb = jax.random.split(ks[5], 3)
    w_branch = jnp.concatenate([
        nrm(kb[0], (DEPTH, A_WIDTH, D_MODEL), A_WIDTH ** -0.5),
        nrm(kb[1], (DEPTH, B_Q_WIDTH, D_MODEL), B_Q_WIDTH ** -0.5),
        nrm(kb[2], (DEPTH, C_WIDTH, D_MODEL), C_WIDTH ** -0.5)], axis=1)
    return {
        'x': nrm(ks[0], (BATCH, SEQ, D_MODEL), 1.0),
        'w_in': nrm(ks[1], (DEPTH, D_MODEL, IN_WIDTH), D_MODEL ** -0.5),
        'b_gate': nrm(ks[2], (DEPTH, N_BRANCH * D_MODEL), 0.1),
        'qk_gain': 1.0 + nrm(ks[3], (DEPTH, N_BRANCH, 2, HEAD_DIM), 0.05),
        'rel_pos_bias': nrm(ks[4], (DEPTH, C_HEADS, 2 * C_WIN_ROWS - 1, 2 * C_WIN_COLS - 1), 0.5),
        'w_branch': w_branch,
        'w_out': nrm(ks[6], (DEPTH, D_MODEL, D_MODEL), D_MODEL ** -0.5),
        'norm_mix': 1.0 + nrm(ks[7], (DEPTH, D_MODEL), 0.05),
        'norm_ffn': 1.0 + nrm(ks[8], (DEPTH, D_MODEL), 0.05),
        'w_up': nrm(ks[9], (DEPTH, D_MODEL, 2 * D_FF), D_MODEL ** -0.5),
        'conv_w': nrm(ks[10], (DEPTH, CONV_WIDTH, 2 * D_FF), CONV_WIDTH ** -0.5),
        'conv_b': nrm(ks[11], (DEPTH, 2 * D_FF), 0.02),
        'w_down': nrm(ks[12], (DEPTH, D_FF, D_MODEL), D_FF ** -0.5),
    }


def reference(x, w_in, b_gate, qk_gain, rel_pos_bias, w_branch, w_out, norm_mix, norm_ffn,
              w_up, conv_w, conv_b, w_down):
    S = x.shape[1]
    t = jnp.arange(S)
    ang_1d = rope_angles(t.astype(jnp.float32), HEAD_DIM)
    ang_2d = jnp.concatenate([
        rope_angles((t // GRID_W).astype(jnp.float32), HEAD_DIM // 2),
        rope_angles((t % GRID_W).astype(jnp.float32), HEAD_DIM // 2)],
        axis=-1)
    for l in range(DEPTH):
        x = hybrid_layer(x, w_in[l], b_gate[l], qk_gain[l], rel_pos_bias[l], w_branch[l],
                         w_out[l], norm_mix[l], norm_ffn[l], w_up[l], conv_w[l], conv_b[l],
                         w_down[l], ang_1d, ang_2d)
    return x
```

```python
import functools

import numpy as np
import jax
import jax.numpy as jnp
from jax import lax
from jax.experimental import pallas as pl
from jax.experimental.pallas import tpu as pltpu

D_MODEL = 1024
GRID_W = 64
HEAD_DIM = 64
HALF = HEAD_DIM // 2
A_HEADS = 4
B_Q_HEADS = 8
B_KV_HEADS = 2
C_HEADS = 4
A_WIDTH = A_HEADS * HEAD_DIM
B_Q_WIDTH = B_Q_HEADS * HEAD_DIM
B_KV_WIDTH = B_KV_HEADS * HEAD_DIM
C_WIDTH = C_HEADS * HEAD_DIM
N_BRANCH = 3
QKV_WIDTH = 3 * A_WIDTH + B_Q_WIDTH + 2 * B_KV_WIDTH + 3 * C_WIDTH
A_PATTERNS = ((128, 1), (512, 4), (2048, 16))
C_WIN_ROWS = 8
C_WIN_COLS = 16
ROPE_THETA = 10000.0
D_FF = 3 * D_MODEL
CONV_WIDTH = 3
EPS = 1e-6
NEG_INF = -1e30

LANES = 128
SUBLANES = 8
VMEM_LIMIT = 56 * 1024 * 1024

TM_PROJ = 512
TQ = 256
A_TILE = 256
C_TILE_ROWS = 4
C_KEY_ROWS = 12
TM_FFN = 1024
FC = 512
HALO = SUBLANES

N_NORMED_CHUNKS = 13
N_V_CHUNKS = 5


def _params(*sem):
    return pltpu.CompilerParams(dimension_semantics=sem, vmem_limit_bytes=VMEM_LIMIT)


def _const_spec(shape):
    nd = len(shape)
    return pl.BlockSpec(shape, lambda *_: (0,) * nd, pipeline_mode=pl.Buffered(1))


def _pair_cols(base, head_a, head_b):
    lane = np.arange(LANES)
    half, slot, i = lane // HEAD_DIM, (lane % HEAD_DIM) // HALF, lane % HALF
    head = np.where(slot == 0, head_a, head_b)
    return base + head * HEAD_DIM + half * HALF + i


def _qkv_column_order():
    o_qa, o_ka, o_va = 0, A_WIDTH, 2 * A_WIDTH
    o_qb = 3 * A_WIDTH
    o_kb = o_qb + B_Q_WIDTH
    o_vb = o_kb + B_KV_WIDTH
    o_qc = o_vb + B_KV_WIDTH
    o_kc, o_vc = o_qc + C_WIDTH, o_qc + 2 * C_WIDTH
    cols = []
    for base in (o_qa, o_ka):
        cols += [_pair_cols(base, 0, 1), _pair_cols(base, 2, 3)]
    group = B_Q_HEADS // B_KV_HEADS
    cols += [_pair_cols(o_qb, j, group + j) for j in range(group)]
    cols += [_pair_cols(o_kb, 0, 1)]
    for base in (o_qc, o_kc):
        cols += [_pair_cols(base, 0, 1), _pair_cols(base, 2, 3)]
    cols += [np.arange(o_va, o_va + A_WIDTH), np.arange(o_vb, o_vb + B_KV_WIDTH),
             np.arange(o_vc, o_vc + C_WIDTH)]
    return np.concatenate(cols)


def _b_out_row_order():
    group = B_Q_HEADS // B_KV_HEADS
    rows = []
    for j in range(group):
        rows += [np.arange(j * HEAD_DIM, (j + 1) * HEAD_DIM),
                 np.arange((group + j) * HEAD_DIM, (group + j + 1) * HEAD_DIM)]
    return np.concatenate(rows)


def _pair_lane_gain_index():
    lane = np.arange(LANES)
    return (lane // HEAD_DIM) * HALF + lane % HALF


def _head_mean_matrix():
    lane = np.arange(LANES)
    slot = (lane % HEAD_DIM) // HALF
    return (slot[:, None] == slot[None, :]).astype(np.float32) / HEAD_DIM


def _dilated_log_multiplicity(seq):
    n = seq // A_TILE
    a = np.arange(A_TILE)
    tiles = []
    for idx in range(2 * n - 1):
        d = (idx - (n - 1)) * A_TILE + a[:, None] - a[None, :]
        count = np.zeros(d.shape, np.int32)
        for window, dil in A_PATTERNS:
            count += ((d % dil == 0) & (np.abs(d) <= window // 2)).astype(np.int32)
        tiles.append(np.where(count > 0, np.log(np.maximum(count, 1).astype(np.float32)),
                              np.float32(NEG_INF)).astype(np.float32))
    return np.stack(tiles)


def _rope_tables(seq):
    t = jnp.arange(seq)

    def angles(pos, dim):
        inv = ROPE_THETA ** (-jnp.arange(0, dim, 2, dtype=jnp.float32) / dim)
        return pos[:, None] * inv[None, :]

    ang_1d = angles(t.astype(jnp.float32), HEAD_DIM)
    ang_2d = jnp.concatenate([angles((t // GRID_W).astype(jnp.float32), HEAD_DIM // 2),
                              angles((t % GRID_W).astype(jnp.float32), HEAD_DIM // 2)], axis=-1)
    lane = np.arange(LANES)
    idx = lane % HALF
    sign = np.where(lane // HEAD_DIM == 0, -1.0, 1.0).astype(np.float32)
    out = []
    for ang in (ang_1d, ang_2d):
        out += [jnp.cos(ang)[:, idx], jnp.sin(ang)[:, idx] * sign]
    return out


def _rms_normed_bf16(x, g):
    ms = jnp.mean(x * x, axis=-1, keepdims=True)
    return (x * lax.rsqrt(ms + EPS) * g).astype(jnp.bfloat16)


def _inproj_kernel(x_ref, g_ref, w_ref, gain_ref, mavg_ref, cos_a_ref, sin_a_ref,
                   cos_b_ref, sin_b_ref,
                   qa_ref, ka_ref, qb_ref, kb_ref, qc_ref, kc_ref, va_ref, vb_ref, vc_ref):
    h = _rms_normed_bf16(x_ref[...], g_ref[...])
    y = jnp.dot(h, w_ref[...], preferred_element_type=jnp.float32)
    mavg = mavg_ref[...]
    scale = HEAD_DIM ** -0.5
    plan = ((qa_ref, 2, (cos_a_ref, sin_a_ref), True), (ka_ref, 2, (cos_a_ref, sin_a_ref), False),
            (qb_ref, 4, (cos_b_ref, sin_b_ref), True), (kb_ref, 1, (cos_b_ref, sin_b_ref), False),
            (qc_ref, 2, None, True), (kc_ref, 2, None, False))
    chunk = 0
    gain_row = 0
    for out_ref, n_chunks, rope, is_q in plan:
        gain = gain_ref[gain_row:gain_row + 1, :]
        for c in range(n_chunks):
            t = y[:, chunk * LANES:(chunk + 1) * LANES]
            ms = jnp.dot((t * t).astype(jnp.bfloat16), mavg, preferred_element_type=jnp.float32)
            t = t * lax.rsqrt(ms + EPS) * gain
            if rope is not None:
                t = t * rope[0][...] + pltpu.roll(t, HEAD_DIM, axis=1) * rope[1][...]
            if is_q:
                t = t * scale
            out_ref[:, c * LANES:(c + 1) * LANES] = t.astype(out_ref.dtype)
            chunk += 1
        gain_row += 1
    for out_ref, n_chunks in ((va_ref, 2), (vb_ref, 1), (vc_ref, 2)):
        width = n_chunks * LANES
        out_ref[...] = y[:, chunk * LANES:chunk * LANES + width].astype(out_ref.dtype)
        chunk += n_chunks


def _in_projection(x2, norm_g, w_qkv, gains, mavg, ropes, seq):
    tokens = x2.shape[0]
    tm = TM_PROJ
    tiles_per_seq = seq // tm
    row = lambda width: pl.BlockSpec((tm, width), lambda i: (i, 0))
    rope_spec = pl.BlockSpec((tm, LANES), lambda i: (i % tiles_per_seq, 0))
    widths = (A_WIDTH, A_WIDTH, B_Q_WIDTH, B_KV_WIDTH, C_WIDTH, C_WIDTH, A_WIDTH, B_KV_WIDTH, C_WIDTH)
    return pl.pallas_call(
        _inproj_kernel,
        grid=(tokens // tm,),
        in_specs=[row(D_MODEL), _const_spec((1, D_MODEL)), _const_spec((D_MODEL, QKV_WIDTH)),
                  _const_spec(gains.shape), _const_spec((LANES, LANES)),
                  rope_spec, rope_spec, rope_spec, rope_spec],
        out_specs=[row(w) for w in widths],
        out_shape=[jax.ShapeDtypeStruct((tokens, w), jnp.bfloat16) for w in widths],
        compiler_params=_params("parallel"),
        name="in_projection",
    )(x2, norm_g, w_qkv, gains, mavg, *ropes)


def _softmax_pv(score_tiles, v_tiles):
    m = score_tiles[0].max(axis=-1, keepdims=True)
    for s in score_tiles[1:]:
        m = jnp.maximum(m, s.max(axis=-1, keepdims=True))
    den = None
    acc = None
    for s, v in zip(score_tiles, v_tiles):
        p = jnp.exp(s - m)
        part = p.sum(axis=-1, keepdims=True)
        pv = jnp.dot(p.astype(jnp.bfloat16), v, preferred_element_type=jnp.float32)
        den = part if den is None else den + part
        acc = pv if acc is None else acc + pv
    return acc / den


def _slot_masks(shape):
    lane = lax.broadcasted_iota(jnp.int32, shape, 1)
    return (lane % HEAD_DIM) < HALF, lane < HEAD_DIM


def _nt_dot(a, b):
    return lax.dot_general(a, b, (((1,), (1,)), ((), ())), preferred_element_type=jnp.float32)


def _dense_attn_kernel(q_ref, k_ref, v_ref, *rest, n_key_tiles):
    if n_key_tiles is None:
        (o_ref,) = rest
        bias_ref = None
    else:
        bias_ref, o_ref = rest
    q = q_ref[...]
    q_head0, out_head0 = _slot_masks(q.shape)
    zero = jnp.zeros_like(q)
    outs = []
    for slot in range(2):
        qm = jnp.where(q_head0 if slot == 0 else ~q_head0, q, zero)
        if bias_ref is None:
            scores = [_nt_dot(qm, k_ref[...])]
            values = [v_ref[...]]
        else:
            qi = pl.program_id(2)
            scores, values = [], []
            for j in range(n_key_tiles):
                rows = slice(j * A_TILE, (j + 1) * A_TILE)
                scores.append(_nt_dot(qm, k_ref[rows, :]) + bias_ref[qi - j + (n_key_tiles - 1)])
                values.append(v_ref[rows, :])
        outs.append(_softmax_pv(scores, values))
    o_ref[...] = jnp.where(out_head0, outs[0], outs[1]).astype(o_ref.dtype)


def _dense_attention(q, k, v, seq, bias=None):
    tokens = q.shape[0]
    batch = tokens // seq
    nq, nk = q.shape[1] // LANES, k.shape[1] // LANES
    q_tiles = seq // TQ
    q_spec = pl.BlockSpec((TQ, LANES), lambda b, c, i: (b * q_tiles + i, c))
    kv_spec = pl.BlockSpec((seq, LANES), lambda b, c, i: (b, c * nk // nq))
    in_specs = [q_spec, kv_spec, kv_spec]
    args = [q, k, v]
    n_key_tiles = None
    if bias is not None:
        assert TQ == A_TILE
        n_key_tiles = seq // A_TILE
        in_specs.append(_const_spec(bias.shape))
        args.append(bias)
    return pl.pallas_call(
        functools.partial(_dense_attn_kernel, n_key_tiles=n_key_tiles),
        grid=(batch, nq, q_tiles),
        in_specs=in_specs,
        out_specs=q_spec,
        out_shape=jax.ShapeDtypeStruct(q.shape, jnp.bfloat16),
        compiler_params=_params("parallel", "parallel", "parallel"),
        name="dilated_attention" if bias is not None else "gqa_attention",
    )(*args)


C_CASE_TILES = (0, 1, None)


def _c_geometry(rows_total):
    n_tiles = rows_total // C_TILE_ROWS
    wr = min(C_WIN_ROWS, rows_total)
    key_units = C_KEY_ROWS // C_TILE_ROWS
    return n_tiles, wr, key_units


def _c_key_unit(i, n_tiles, key_units):
    return jnp.clip(i - 1, 0, n_tiles - key_units)


def _nbr_bias_kernel(rpb_ref, o_ref, *, rows_total):
    n_tiles, wr, key_units = _c_geometry(rows_total)
    h = pl.program_id(0)
    shape = (GRID_W, LANES)
    c = lax.broadcasted_iota(jnp.int32, shape, 0)
    lane = lax.broadcasted_iota(jnp.int32, shape, 1)
    kc = lane % GRID_W
    d = kc - c
    start = jnp.clip(c - C_WIN_COLS // 2, 0, GRID_W - C_WIN_COLS)
    in_window = (kc >= start) & (kc < start + C_WIN_COLS)
    neg = jnp.full(shape, NEG_INF, jnp.float32)
    n_dr = 2 * C_WIN_ROWS - 1
    n_dc = 2 * C_WIN_COLS - 1
    col_bias = []
    for dri in range(n_dr):
        t = neg
        for j in range(n_dc):
            t = jnp.where(d == j - (C_WIN_COLS - 1), rpb_ref[h * n_dr + dri, j], t)
        col_bias.append(jnp.where(in_window, t, neg))
    left = lane < GRID_W
    for case, tile in enumerate((0, 1, n_tiles - 1)):
        key_row0 = C_TILE_ROWS * min(max(tile - 1, 0), n_tiles - key_units)
        for rq in range(C_TILE_ROWS):
            r = tile * C_TILE_ROWS + rq
            first = min(max(r - wr // 2, 0), rows_total - wr)
            for pair in range(C_KEY_ROWS // 2):
                halves = []
                for e in range(2):
                    kr = key_row0 + 2 * pair + e
                    halves.append(col_bias[kr - r + C_WIN_ROWS - 1] if first <= kr < first + wr else neg)
                o_ref[case, 0, rq * GRID_W:(rq + 1) * GRID_W, pair * LANES:(pair + 1) * LANES] = (
                    jnp.where(left, halves[0], halves[1]))


def _nbr_bias_table(rpb, rows_total):
    heads = rpb.shape[0]
    n_dr, n_dc = rpb.shape[1], rpb.shape[2]
    tq = C_TILE_ROWS * GRID_W
    tk = C_KEY_ROWS * GRID_W
    return pl.pallas_call(
        functools.partial(_nbr_bias_kernel, rows_total=rows_total),
        grid=(heads,),
        in_specs=[pl.BlockSpec(memory_space=pltpu.SMEM)],
        out_specs=pl.BlockSpec((3, 1, tq, tk), lambda h: (0, h, 0, 0)),
        out_shape=jax.ShapeDtypeStruct((3, heads, tq, tk), jnp.float32),
        compiler_params=_params("parallel"),
        name="nbr_bias_table",
    )(rpb.reshape(heads * n_dr, n_dc))


def _nbr_attn_kernel(q_ref, *rest, key_units):
    k_refs = rest[:key_units]
    v_refs = rest[key_units:2 * key_units]
    bias_ref, o_ref = rest[2 * key_units:]
    tk = k_refs[0].shape[0]
    n_chunks = q_ref.shape[1] // LANES
    q_head0, out_head0 = _slot_masks((q_ref.shape[0], LANES))
    for c in range(n_chunks):
        lanes = slice(c * LANES, (c + 1) * LANES)
        q = q_ref[:, lanes]
        zero = jnp.zeros_like(q)
        outs = []
        for slot in range(2):
            qm = jnp.where(q_head0 if slot == 0 else ~q_head0, q, zero)
            head = 2 * c + slot
            scores = [_nt_dot(qm, k_refs[u][:, lanes]) + bias_ref[0, head, :, u * tk:(u + 1) * tk]
                      for u in range(key_units)]
            values = [v_refs[u][:, lanes] for u in range(key_units)]
            outs.append(_softmax_pv(scores, values))
        o_ref[:, lanes] = jnp.where(out_head0, outs[0], outs[1]).astype(o_ref.dtype)


def _nbr_attention(q, k, v, bias, seq):
    tokens, width = q.shape
    batch = tokens // seq
    rows_total = seq // GRID_W
    n_tiles, _, key_units = _c_geometry(rows_total)
    tq = C_TILE_ROWS * GRID_W

    def case_of(i):
        return jnp.where(i == 0, 0, jnp.where(i == n_tiles - 1, 2, 1))

    q_spec = pl.BlockSpec((tq, width), lambda i, b: (b * n_tiles + i, 0))

    def key_spec(u):
        return pl.BlockSpec((tq, width), lambda i, b: (b * n_tiles + _c_key_unit(i, n_tiles, key_units) + u, 0))

    kv_specs = [key_spec(u) for u in range(key_units)]
    bias_spec = pl.BlockSpec((1,) + bias.shape[1:], lambda i, b: (case_of(i), 0, 0, 0))
    return pl.pallas_call(
        functools.partial(_nbr_attn_kernel, key_units=key_units),
        grid=(n_tiles, batch),
        in_specs=[q_spec] + kv_specs + kv_specs + [bias_spec],
        out_specs=q_spec,
        out_shape=jax.ShapeDtypeStruct(q.shape, jnp.bfloat16),
        compiler_params=_params("parallel", "parallel"),
        name="nbr_attention",
    )(q, *([k] * key_units), *([v] * key_units), bias)


def _merge_kernel(x_ref, g_ref, oa_ref, ob_ref, oc_ref, wg_ref, bg_ref, wa_ref, wb_ref, wc_ref,
                  wo_ref, o_ref):
    x = x_ref[...]
    h = _rms_normed_bf16(x, g_ref[...])
    merged = None
    for n, (br_ref, w_ref) in enumerate(((oa_ref, wa_ref), (ob_ref, wb_ref), (oc_ref, wc_ref))):
        cols = slice(n * D_MODEL, (n + 1) * D_MODEL)
        logits = jnp.dot(h, wg_ref[:, cols], preferred_element_type=jnp.float32) + bg_ref[:, cols]
        gate = 1.0 / (1.0 + jnp.exp(-logits))
        y = jnp.dot(br_ref[...], w_ref[...], preferred_element_type=jnp.float32)
        merged = gate * y if merged is None else merged + gate * y
    o_ref[...] = x + jnp.dot(merged.astype(jnp.bfloat16), wo_ref[...], preferred_element_type=jnp.float32)


def _merge(x2, norm_g, oa, ob, oc, w_gate, b_gate, wa, wb, wc, w_out):
    tokens = x2.shape[0]
    tm = TM_PROJ
    row = lambda width: pl.BlockSpec((tm, width), lambda i: (i, 0))
    consts = (norm_g,)
    return pl.pallas_call(
        _merge_kernel,
        grid=(tokens // tm,),
        in_specs=[row(D_MODEL), _const_spec(norm_g.shape), row(oa.shape[1]), row(ob.shape[1]),
                  row(oc.shape[1]), _const_spec(w_gate.shape), _const_spec(b_gate.shape),
                  _const_spec(wa.shape), _const_spec(wb.shape), _const_spec(wc.shape),
                  _const_spec(w_out.shape)],
        out_specs=row(D_MODEL),
        out_shape=jax.ShapeDtypeStruct(x2.shape, jnp.float32),
        compiler_params=_params("parallel"),
        name="gated_merge",
    )(x2, norm_g, oa, ob, oc, w_gate, b_gate, wa, wb, wc, w_out)


def _gelu_tanh(x):
    return 0.5 * x * (1.0 + jnp.tanh(np.sqrt(2.0 / np.pi).astype(np.float32) * (x + 0.044715 * (x * x * x))))


def _ffn_kernel(x_ref, prev_ref, next_ref, g_ref, wg_ref, wv_ref, cwg_ref, cwv_ref, cbg_ref, cbv_ref,
                wd_ref, o_ref, h_ref, *, tiles_per_seq):
    i = pl.program_id(0)
    f = pl.program_id(1)
    tm = x_ref.shape[0]

    @pl.when(f == 0)
    def _():
        g = g_ref[...]
        has_prev = (i % tiles_per_seq != 0).astype(jnp.float32)
        has_next = (i % tiles_per_seq != tiles_per_seq - 1).astype(jnp.float32)
        h_ref[0:HALO, :] = _rms_normed_bf16(prev_ref[...] * has_prev, g)
        h_ref[HALO:HALO + tm, :] = _rms_normed_bf16(x_ref[...], g)
        h_ref[HALO + tm:, :] = _rms_normed_bf16(next_ref[...] * has_next, g)
        o_ref[...] = x_ref[...]

    h = h_ref[...]

    def conv(w_ref, cw_ref, cb_ref):
        u = jnp.dot(h, w_ref[...], preferred_element_type=jnp.float32)
        out = cb_ref[...] + u[HALO - 1:HALO - 1 + tm] * cw_ref[0:1, :]
        for j in range(1, CONV_WIDTH):
            out = out + u[HALO - 1 + j:HALO - 1 + j + tm] * cw_ref[j:j + 1, :]
        return out

    act = _gelu_tanh(conv(wg_ref, cwg_ref, cbg_ref)) * conv(wv_ref, cwv_ref, cbv_ref)
    o_ref[...] += jnp.dot(act.astype(jnp.bfloat16), wd_ref[...], preferred_element_type=jnp.float32)


def _ffn(x2, norm_g, w_up, conv_w, conv_b, w_down, seq):
    tokens = x2.shape[0]
    tm = TM_FFN
    tiles_per_seq = seq // tm
    nf = D_FF // FC
    halo_blocks = tm // HALO
    last_halo = tokens // HALO - 1
    in_specs = [
        pl.BlockSpec((tm, D_MODEL), lambda i, f: (i, 0)),
        pl.BlockSpec((HALO, D_MODEL), lambda i, f: (jnp.maximum(i * halo_blocks - 1, 0), 0)),
        pl.BlockSpec((HALO, D_MODEL), lambda i, f: (jnp.minimum((i + 1) * halo_blocks, last_halo), 0)),
        pl.BlockSpec((1, D_MODEL), lambda i, f: (0, 0)),
        pl.BlockSpec((D_MODEL, FC), lambda i, f: (0, f)),
        pl.BlockSpec((D_MODEL, FC), lambda i, f: (0, nf + f)),
        pl.BlockSpec((CONV_WIDTH, FC), lambda i, f: (0, f)),
        pl.BlockSpec((CONV_WIDTH, FC), lambda i, f: (0, nf + f)),
        pl.BlockSpec((1, FC), lambda i, f: (0, f)),
        pl.BlockSpec((1, FC), lambda i, f: (0, nf + f)),
        pl.BlockSpec((FC, D_MODEL), lambda i, f: (f, 0)),
    ]
    return pl.pallas_call(
        functools.partial(_ffn_kernel, tiles_per_seq=tiles_per_seq),
        grid=(tokens // tm, nf),
        in_specs=in_specs,
        out_specs=pl.BlockSpec((tm, D_MODEL), lambda i, f: (i, 0)),
        out_shape=jax.ShapeDtypeStruct(x2.shape, jnp.float32),
        scratch_shapes=[pltpu.VMEM((tm + 2 * HALO, D_MODEL), jnp.bfloat16)],
        compiler_params=_params("parallel", "arbitrary"),
        name="ffn",
    )(x2, x2, x2, norm_g, w_up, w_up, conv_w, conv_w, conv_b, conv_b, w_down)


def kernel(x, w_in, b_gate, qk_gain, rel_pos_bias, w_branch, w_out, norm_mix, norm_ffn, w_up, conv_w,
           conv_b, w_down):
    batch, seq, d_model = x.shape
    depth = w_in.shape[0]
    assert d_model == D_MODEL and seq % TM_FFN == 0 and seq % A_TILE == 0
    bf16 = jnp.bfloat16
    x2 = x.reshape(batch * seq, d_model)

    col_order = _qkv_column_order()
    gain_idx = _pair_lane_gain_index()
    mavg = jnp.asarray(_head_mean_matrix(), bf16)
    ropes = _rope_tables(seq)
    log_mult = jnp.asarray(_dilated_log_multiplicity(seq))
    b_rows = A_WIDTH + _b_out_row_order()

    for l in range(depth):
        w_qkv = w_in[l][:, col_order].astype(bf16)
        w_gate = w_in[l][:, QKV_WIDTH:].astype(bf16)
        gains = qk_gain[l].reshape(2 * N_BRANCH, HEAD_DIM)[:, gain_idx]
        wbr = w_branch[l]
        wa = wbr[:A_WIDTH].astype(bf16)
        wb = wbr[b_rows].astype(bf16)
        wc = wbr[A_WIDTH + B_Q_WIDTH:].astype(bf16)

        qa, ka, qb, kb, qc, kc, va, vb, vc = _in_projection(
            x2, norm_mix[l][None, :], w_qkv, gains, mavg, ropes, seq)
        oa = _dense_attention(qa, ka, va, seq, bias=log_mult)
        ob = _dense_attention(qb, kb, vb, seq)
        nbr_bias = _nbr_bias_table(rel_pos_bias[l], seq // GRID_W)
        oc = _nbr_attention(qc, kc, vc, nbr_bias, seq)
        x2 = _merge(x2, norm_mix[l][None, :], oa, ob, oc, w_gate, b_gate[l][None, :], wa, wb, wc,
                    w_out[l].astype(bf16))
        x2 = _ffn(x2, norm_ffn[l][None, :], w_up[l].astype(bf16), conv_w[l], conv_b[l][None, :],
                  w_down[l].astype(bf16), seq)
    return x2.reshape(batch, seq, d_model)
```

```python
import functools

import numpy as np
import jax
import jax.numpy as jnp
from jax import lax
from jax.experimental import pallas as pl
from jax.experimental.pallas import tpu as pltpu

D_MODEL = 1024
GRID_W = 64
HEAD_DIM = 64
HALF = HEAD_DIM // 2
A_HEADS = 4
B_Q_HEADS = 8
B_KV_HEADS = 2
C_HEADS = 4
A_WIDTH = A_HEADS * HEAD_DIM
B_Q_WIDTH = B_Q_HEADS * HEAD_DIM
B_KV_WIDTH = B_KV_HEADS * HEAD_DIM
C_WIDTH = C_HEADS * HEAD_DIM
N_BRANCH = 3
QKV_WIDTH = 3 * A_WIDTH + B_Q_WIDTH + 2 * B_KV_WIDTH + 3 * C_WIDTH
A_PATTERNS = ((128, 1), (512, 4), (2048, 16))
C_WIN_ROWS = 8
C_WIN_COLS = 16
ROPE_THETA = 10000.0
D_FF = 3 * D_MODEL
CONV_WIDTH = 3
EPS = 1e-6
NEG_INF = -1e30
LOG2_E = float(np.log2(np.e))

LANES = 128
SUBLANES = 8
VMEM_LIMIT = 56 * 1024 * 1024

TM_PROJ = 512
TQ = 256
A_TILE = 256
C_TILE_ROWS = 4
C_KEY_ROWS = 12
TM_FFN = 1024
FC = 1024
FC_SUB = 256
FC_DOWN = 512
HALO = SUBLANES

N_NORMED_CHUNKS = 13
N_V_CHUNKS = 5


def _params(*sem):
    return pltpu.CompilerParams(dimension_semantics=sem, vmem_limit_bytes=VMEM_LIMIT)


def _const_spec(shape):
    nd = len(shape)
    return pl.BlockSpec(shape, lambda *_: (0,) * nd, pipeline_mode=pl.Buffered(1))


def _pair_cols(base, head_a, head_b):
    lane = np.arange(LANES)
    half, slot, i = lane // HEAD_DIM, (lane % HEAD_DIM) // HALF, lane % HALF
    head = np.where(slot == 0, head_a, head_b)
    return base + head * HEAD_DIM + half * HALF + i


def _qkv_column_order():
    o_qa, o_ka, o_va = 0, A_WIDTH, 2 * A_WIDTH
    o_qb = 3 * A_WIDTH
    o_kb = o_qb + B_Q_WIDTH
    o_vb = o_kb + B_KV_WIDTH
    o_qc = o_vb + B_KV_WIDTH
    o_kc, o_vc = o_qc + C_WIDTH, o_qc + 2 * C_WIDTH
    cols = []
    for base in (o_qa, o_ka):
        cols += [_pair_cols(base, 0, 1), _pair_cols(base, 2, 3)]
    group = B_Q_HEADS // B_KV_HEADS
    cols += [_pair_cols(o_qb, j, group + j) for j in range(group)]
    cols += [_pair_cols(o_kb, 0, 1)]
    for base in (o_qc, o_kc):
        cols += [_pair_cols(base, 0, 1), _pair_cols(base, 2, 3)]
    cols += [np.arange(o_va, o_va + A_WIDTH), np.arange(o_vb, o_vb + B_KV_WIDTH),
             np.arange(o_vc, o_vc + C_WIDTH)]
    return np.concatenate(cols)


def _b_out_row_order():
    group = B_Q_HEADS // B_KV_HEADS
    rows = []
    for j in range(group):
        rows += [np.arange(j * HEAD_DIM, (j + 1) * HEAD_DIM),
                 np.arange((group + j) * HEAD_DIM, (group + j + 1) * HEAD_DIM)]
    return np.concatenate(rows)


def _pair_lane_gain_index():
    lane = np.arange(LANES)
    return (lane // HEAD_DIM) * HALF + lane % HALF


def _head_mean_matrix():
    lane = np.arange(LANES)
    slot = (lane % HEAD_DIM) // HALF
    return (slot[:, None] == slot[None, :]).astype(np.float32) / HEAD_DIM


def _dilated_log_multiplicity(seq):
    n = seq // A_TILE
    a = np.arange(A_TILE)
    tiles = []
    for idx in range(2 * n - 1):
        d = (idx - (n - 1)) * A_TILE + a[:, None] - a[None, :]
        count = np.zeros(d.shape, np.int32)
        for window, dil in A_PATTERNS:
            count += ((d % dil == 0) & (np.abs(d) <= window // 2)).astype(np.int32)
        tiles.append(np.where(count > 0, np.log2(np.maximum(count, 1).astype(np.float32)),
                              np.float32(NEG_INF)).astype(np.float32))
    return np.stack(tiles)


def _rope_tables(seq):
    t = jnp.arange(seq)

    def angles(pos, dim):
        inv = ROPE_THETA ** (-jnp.arange(0, dim, 2, dtype=jnp.float32) / dim)
        return pos[:, None] * inv[None, :]

    ang_1d = angles(t.astype(jnp.float32), HEAD_DIM)
    ang_2d = jnp.concatenate([angles((t // GRID_W).astype(jnp.float32), HEAD_DIM // 2),
                              angles((t % GRID_W).astype(jnp.float32), HEAD_DIM // 2)], axis=-1)
    lane = np.arange(LANES)
    idx = lane % HALF
    sign = np.where(lane // HEAD_DIM == 0, -1.0, 1.0).astype(np.float32)
    out = []
    for ang in (ang_1d, ang_2d):
        out += [jnp.cos(ang)[:, idx], jnp.sin(ang)[:, idx] * sign]
    return out


def _rms_normed_bf16(x, g):
    ms = jnp.mean(x * x, axis=-1, keepdims=True)
    return (x * lax.rsqrt(ms + EPS) * g).astype(jnp.bfloat16)


def _inproj_kernel(x_ref, g_ref, w_ref, gain_ref, mavg_ref, cos_a_ref, sin_a_ref,
                   cos_b_ref, sin_b_ref,
                   qa_ref, ka_ref, qb_ref, kb_ref, qc_ref, kc_ref, va_ref, vb_ref, vc_ref):
    h = _rms_normed_bf16(x_ref[...], g_ref[...])
    y = jnp.dot(h, w_ref[...], preferred_element_type=jnp.float32)
    mavg = mavg_ref[...]
    scale = HEAD_DIM ** -0.5 * LOG2_E
    plan = ((qa_ref, 2, (cos_a_ref, sin_a_ref), True), (ka_ref, 2, (cos_a_ref, sin_a_ref), False),
            (qb_ref, 4, (cos_b_ref, sin_b_ref), True), (kb_ref, 1, (cos_b_ref, sin_b_ref), False),
            (qc_ref, 2, None, True), (kc_ref, 2, None, False))
    chunk = 0
    gain_row = 0
    for out_ref, n_chunks, rope, is_q in plan:
        gain = gain_ref[gain_row:gain_row + 1, :]
        for c in range(n_chunks):
            t = y[:, chunk * LANES:(chunk + 1) * LANES]
            ms = jnp.dot((t * t).astype(jnp.bfloat16), mavg, preferred_element_type=jnp.float32)
            t = t * lax.rsqrt(ms + EPS) * gain
            if rope is not None:
                t = t * rope[0][...] + pltpu.roll(t, HEAD_DIM, axis=1) * rope[1][...]
            if is_q:
                t = t * scale
            out_ref[:, c * LANES:(c + 1) * LANES] = t.astype(out_ref.dtype)
            chunk += 1
        gain_row += 1
    for out_ref, n_chunks in ((va_ref, 2), (vb_ref, 1), (vc_ref, 2)):
        width = n_chunks * LANES
        out_ref[...] = y[:, chunk * LANES:chunk * LANES + width].astype(out_ref.dtype)
        chunk += n_chunks


def _in_projection(x2, norm_g, w_qkv, gains, mavg, ropes, seq):
    tokens = x2.shape[0]
    tm = TM_PROJ
    tiles_per_seq = seq // tm
    row = lambda width: pl.BlockSpec((tm, width), lambda i: (i, 0))
    rope_spec = pl.BlockSpec((tm, LANES), lambda i: (i % tiles_per_seq, 0))
    widths = (A_WIDTH, A_WIDTH, B_Q_WIDTH, B_KV_WIDTH, C_WIDTH, C_WIDTH, A_WIDTH, B_KV_WIDTH, C_WIDTH)
    return pl.pallas_call(
        _inproj_kernel,
        grid=(tokens // tm,),
        in_specs=[row(D_MODEL), _const_spec((1, D_MODEL)), _const_spec((D_MODEL, QKV_WIDTH)),
                  _const_spec(gains.shape), _const_spec((LANES, LANES)),
                  rope_spec, rope_spec, rope_spec, rope_spec],
        out_specs=[row(w) for w in widths],
        out_shape=[jax.ShapeDtypeStruct((tokens, w), jnp.bfloat16) for w in widths],
        compiler_params=_params("parallel"),
        name="in_projection",
    )(x2, norm_g, w_qkv, gains, mavg, *ropes)


def _pipelined_softmax_attention(n_units, score_fn, value_fn, emit_fn, s_ref):
    m_next = score_fn(0, s_ref.at[0])
    for u in range(n_units):
        m = m_next
        if u + 1 < n_units:
            m_next = score_fn(u + 1, s_ref.at[(u + 1) % 2])
        p = jnp.exp2(s_ref[u % 2] - m)
        den = p.sum(axis=-1, keepdims=True)
        pv = jnp.dot(p.astype(jnp.bfloat16), value_fn(u), preferred_element_type=jnp.float32)
        emit_fn(u, pv / den)


def _slot_masks(shape):
    lane = lax.broadcasted_iota(jnp.int32, shape, 1)
    return (lane % HEAD_DIM) < HALF, lane < HEAD_DIM


def _nt_dot(a, b):
    return lax.dot_general(a, b, (((1,), (1,)), ((), ())), preferred_element_type=jnp.float32)


def _dense_attn_kernel(q_ref, k_ref, v_ref, *rest, n_key_tiles):
    if n_key_tiles is None:
        o_ref, s_ref = rest
        bias_ref = None
    else:
        bias_ref, o_ref, s_ref = rest
    nq, nk = q_ref.shape[1] // LANES, k_ref.shape[1] // LANES
    q_head0, out_head0 = _slot_masks((q_ref.shape[0], LANES))

    def kv_lanes(h):
        kv_chunk = (h // 2) * nk // nq
        return slice(kv_chunk * LANES, (kv_chunk + 1) * LANES)

    def scores(h, buf):
        c, slot = divmod(h, 2)
        q = q_ref[:, c * LANES:(c + 1) * LANES]
        qm = jnp.where(q_head0 if slot == 0 else ~q_head0, q, jnp.zeros_like(q))
        if bias_ref is None:
            buf[...] = _nt_dot(qm, k_ref[:, kv_lanes(h)])
        else:
            qi = pl.program_id(1)
            for j in range(n_key_tiles):
                rows = slice(j * A_TILE, (j + 1) * A_TILE)
                buf[:, rows] = (_nt_dot(qm, k_ref[rows, kv_lanes(h)])
                                + bias_ref[qi - j + (n_key_tiles - 1)])
        return buf[...].max(axis=-1, keepdims=True)

    pair = []

    def emit(h, out):
        pair.append(out)
        if h % 2 == 1:
            c = h // 2
            o_ref[:, c * LANES:(c + 1) * LANES] = jnp.where(out_head0, *pair).astype(o_ref.dtype)
            pair.clear()

    _pipelined_softmax_attention(2 * nq, scores, lambda h: v_ref[:, kv_lanes(h)], emit, s_ref)


def _dense_attention(q, k, v, seq, bias=None):
    tokens = q.shape[0]
    batch = tokens // seq
    q_tiles = seq // TQ
    q_spec = pl.BlockSpec((TQ, q.shape[1]), lambda b, i: (b * q_tiles + i, 0))
    kv_spec = pl.BlockSpec((seq, k.shape[1]), lambda b, i: (b, 0))
    in_specs = [q_spec, kv_spec, kv_spec]
    args = [q, k, v]
    n_key_tiles = None
    if bias is not None:
        assert TQ == A_TILE
        n_key_tiles = seq // A_TILE
        in_specs.append(_const_spec(bias.shape))
        args.append(bias)
    return pl.pallas_call(
        functools.partial(_dense_attn_kernel, n_key_tiles=n_key_tiles),
        grid=(batch, q_tiles),
        in_specs=in_specs,
        out_specs=q_spec,
        out_shape=jax.ShapeDtypeStruct(q.shape, jnp.bfloat16),
        scratch_shapes=[pltpu.VMEM((2, TQ, seq), jnp.float32)],
        compiler_params=_params("parallel", "parallel"),
        name="dilated_attention" if bias is not None else "gqa_attention",
    )(*args)


def _c_geometry(rows_total):
    n_tiles = rows_total // C_TILE_ROWS
    wr = min(C_WIN_ROWS, rows_total)
    key_units = C_KEY_ROWS // C_TILE_ROWS
    return n_tiles, wr, key_units


def _nbr_bias_kernel(rpb_ref, o_ref, *, rows_total):
    n_tiles, wr, key_units = _c_geometry(rows_total)
    h = pl.program_id(0)
    shape = (GRID_W, LANES)
    c = lax.broadcasted_iota(jnp.int32, shape, 0)
    lane = lax.broadcasted_iota(jnp.int32, shape, 1)
    kc = lane % GRID_W
    d = kc - c
    start = jnp.clip(c - C_WIN_COLS // 2, 0, GRID_W - C_WIN_COLS)
    in_window = (kc >= start) & (kc < start + C_WIN_COLS)
    neg = jnp.full(shape, NEG_INF, jnp.float32)
    n_dr = 2 * C_WIN_ROWS - 1
    n_dc = 2 * C_WIN_COLS - 1
    col_bias = []
    for dri in range(n_dr):
        t = neg
        for j in range(n_dc):
            t = jnp.where(d == j - (C_WIN_COLS - 1), rpb_ref[h * n_dr + dri, j], t)
        col_bias.append(jnp.where(in_window, t * LOG2_E, neg))
    left = lane < GRID_W
    for case, tile in enumerate((0, 1, n_tiles - 1)):
        key_row0 = C_TILE_ROWS * min(max(tile - 1, 0), n_tiles - key_units)
        for rq in range(C_TILE_ROWS):
            r = tile * C_TILE_ROWS + rq
            first = min(max(r - wr // 2, 0), rows_total - wr)
            for pair in range(C_KEY_ROWS // 2):
                halves = []
                for e in range(2):
                    kr = key_row0 + 2 * pair + e
                    halves.append(col_bias[kr - r + C_WIN_ROWS - 1] if first <= kr < first + wr else neg)
                o_ref[case, 0, rq * GRID_W:(rq + 1) * GRID_W, pair * LANES:(pair + 1) * LANES] = (
                    jnp.where(left, halves[0], halves[1]))


def _nbr_bias_table(rpb, rows_total):
    heads = rpb.shape[0]
    n_dr, n_dc = rpb.shape[1], rpb.shape[2]
    tq = C_TILE_ROWS * GRID_W
    tk = C_KEY_ROWS * GRID_W
    return pl.pallas_call(
        functools.partial(_nbr_bias_kernel, rows_total=rows_total),
        grid=(heads,),
        in_specs=[pl.BlockSpec(memory_space=pltpu.SMEM)],
        out_specs=pl.BlockSpec((3, 1, tq, tk), lambda h: (0, h, 0, 0)),
        out_shape=jax.ShapeDtypeStruct((3, heads, tq, tk), jnp.float32),
        compiler_params=_params("parallel"),
        name="nbr_bias_table",
    )(rpb.reshape(heads * n_dr, n_dc))


def _nbr_attn_kernel(q_ref, k_ref, v_ref, bias_ref, o_ref, s_ref, *, rows_total):
    n_tiles, _, key_units = _c_geometry(rows_total)
    tq = C_TILE_ROWS * GRID_W
    n_heads = 2 * (q_ref.shape[1] // LANES)
    q_head0, out_head0 = _slot_masks((tq, LANES))

    def where(u):
        tile, head = divmod(u, n_heads)
        key0 = tq * min(max(tile - 1, 0), n_tiles - key_units)
        case = 0 if tile == 0 else (2 if tile == n_tiles - 1 else 1)
        lanes = slice((head // 2) * LANES, (head // 2 + 1) * LANES)
        return tile, head, slice(key0, key0 + key_units * tq), case, lanes

    def scores(u, buf):
        tile, head, keys, case, lanes = where(u)
        q = q_ref[tile * tq:(tile + 1) * tq, lanes]
        qm = jnp.where(q_head0 if head % 2 == 0 else ~q_head0, q, jnp.zeros_like(q))
        buf[...] = _nt_dot(qm, k_ref[keys, lanes]) + bias_ref[case, head]
        return buf[...].max(axis=-1, keepdims=True)

    def values(u):
        _, _, keys, _, lanes = where(u)
        return v_ref[keys, lanes]

    pair = []

    def emit(u, out):
        tile, head, _, _, lanes = where(u)
        pair.append(out)
        if head % 2 == 1:
            o_ref[tile * tq:(tile + 1) * tq, lanes] = jnp.where(out_head0, *pair).astype(o_ref.dtype)
            pair.clear()

    _pipelined_softmax_attention(n_tiles * n_heads, scores, values, emit, s_ref)


def _nbr_attention(q, k, v, bias, seq):
    tokens, width = q.shape
    rows_total = seq // GRID_W
    _, _, key_units = _c_geometry(rows_total)
    tq = C_TILE_ROWS * GRID_W
    seq_spec = pl.BlockSpec((seq, width), lambda b: (b, 0))
    return pl.pallas_call(
        functools.partial(_nbr_attn_kernel, rows_total=rows_total),
        grid=(tokens // seq,),
        in_specs=[seq_spec, seq_spec, seq_spec, _const_spec(bias.shape)],
        out_specs=seq_spec,
        out_shape=jax.ShapeDtypeStruct(q.shape, jnp.bfloat16),
        scratch_shapes=[pltpu.VMEM((2, tq, key_units * tq), jnp.float32)],
        compiler_params=_params("parallel"),
        name="nbr_attention",
    )(q, k, v, bias)


def _merge_kernel(x_ref, g_ref, oa_ref, ob_ref, oc_ref, wg_ref, bg_ref, wa_ref, wb_ref, wc_ref,
                  wo_ref, o_ref):
    x = x_ref[...]
    h = _rms_normed_bf16(x, g_ref[...])
    merged = None
    for n, (br_ref, w_ref) in enumerate(((oa_ref, wa_ref), (ob_ref, wb_ref), (oc_ref, wc_ref))):
        cols = slice(n * D_MODEL, (n + 1) * D_MODEL)
        logits = jnp.dot(h, wg_ref[:, cols], preferred_element_type=jnp.float32) + bg_ref[:, cols]
        gate = 1.0 / (1.0 + jnp.exp(-logits))
        y = jnp.dot(br_ref[...], w_ref[...], preferred_element_type=jnp.float32)
        merged = gate * y if merged is None else merged + gate * y
    o_ref[...] = x + jnp.dot(merged.astype(jnp.bfloat16), wo_ref[...], preferred_element_type=jnp.float32)


def _merge(x2, norm_g, oa, ob, oc, w_gate, b_gate, wa, wb, wc, w_out):
    tokens = x2.shape[0]
    tm = TM_PROJ
    row = lambda width: pl.BlockSpec((tm, width), lambda i: (i, 0))
    consts = (norm_g,)
    return pl.pallas_call(
        _merge_kernel,
        grid=(tokens // tm,),
        in_specs=[row(D_MODEL), _const_spec(norm_g.shape), row(oa.shape[1]), row(ob.shape[1]),
                  row(oc.shape[1]), _const_spec(w_gate.shape), _const_spec(b_gate.shape),
                  _const_spec(wa.shape), _const_spec(wb.shape), _const_spec(wc.shape),
                  _const_spec(w_out.shape)],
        out_specs=row(D_MODEL),
        out_shape=jax.ShapeDtypeStruct(x2.shape, jnp.float32),
        compiler_params=_params("parallel"),
        name="gated_merge",
    )(x2, norm_g, oa, ob, oc, w_gate, b_gate, wa, wb, wc, w_out)


def _gelu_tanh(x):
    return 0.5 * x * (1.0 + jnp.tanh(np.sqrt(2.0 / np.pi).astype(np.float32) * (x + 0.044715 * (x * x * x))))


def _ffn_kernel(x_ref, prev_ref, next_ref, g_ref, wg_ref, wv_ref, cwg_ref, cwv_ref, cbg_ref, cbv_ref,
                wd_ref, o_ref, h_ref, u_ref, *, tiles_per_seq):
    i = pl.program_id(0)
    f = pl.program_id(1)
    tm = x_ref.shape[0]

    @pl.when(f == 0)
    def _():
        g = g_ref[...]
        has_prev = (i % tiles_per_seq != 0).astype(jnp.float32)
        has_next = (i % tiles_per_seq != tiles_per_seq - 1).astype(jnp.float32)
        h_ref[0:HALO, :] = _rms_normed_bf16(prev_ref[...] * has_prev, g)
        h_ref[HALO:HALO + tm, :] = _rms_normed_bf16(x_ref[...], g)
        h_ref[HALO + tm:, :] = _rms_normed_bf16(next_ref[...] * has_next, g)
        o_ref[...] = x_ref[...]

    h = h_ref[...]

    def conv(w_ref, cw_ref, cb_ref, start, u_scr):
        u = jnp.dot(h, w_ref[:, start:start + FC_SUB], preferred_element_type=jnp.float32)
        outs = []
        for ct in range(FC_SUB // LANES):
            u_scr[ct] = u[:, ct * LANES:(ct + 1) * LANES]
            cols = slice(start + ct * LANES, start + (ct + 1) * LANES)
            out = cb_ref[:, cols] + u_scr[ct, pl.ds(HALO - 1, tm), :] * cw_ref[0:1, cols]
            for j in range(1, CONV_WIDTH):
                out = out + u_scr[ct, pl.ds(HALO - 1 + j, tm), :] * cw_ref[j:j + 1, cols]
            outs.append(out)
        return jnp.concatenate(outs, axis=1)

    for pair in range(FC // FC_DOWN):
        acts = []
        for s in range(FC_DOWN // FC_SUB):
            n = pair * (FC_DOWN // FC_SUB) + s
            start = n * FC_SUB
            act = (_gelu_tanh(conv(wg_ref, cwg_ref, cbg_ref, start, u_ref.at[n, 0]))
                   * conv(wv_ref, cwv_ref, cbv_ref, start, u_ref.at[n, 1]))
            acts.append(act.astype(jnp.bfloat16))
        rows = slice(pair * FC_DOWN, (pair + 1) * FC_DOWN)
        o_ref[...] += jnp.dot(jnp.concatenate(acts, axis=1), wd_ref[rows, :],
                              preferred_element_type=jnp.float32)


def _ffn(x2, norm_g, w_up, conv_w, conv_b, w_down, seq):
    tokens = x2.shape[0]
    tm = TM_FFN
    tiles_per_seq = seq // tm
    nf = D_FF // FC
    halo_blocks = tm // HALO
    last_halo = tokens // HALO - 1
    in_specs = [
        pl.BlockSpec((tm, D_MODEL), lambda i, f: (i, 0)),
        pl.BlockSpec((HALO, D_MODEL), lambda i, f: (jnp.maximum(i * halo_blocks - 1, 0), 0)),
        pl.BlockSpec((HALO, D_MODEL), lambda i, f: (jnp.minimum((i + 1) * halo_blocks, last_halo), 0)),
        pl.BlockSpec((1, D_MODEL), lambda i, f: (0, 0)),
        pl.BlockSpec((D_MODEL, FC), lambda i, f: (0, f)),
        pl.BlockSpec((D_MODEL, FC), lambda i, f: (0, nf + f)),
        pl.BlockSpec((CONV_WIDTH, FC), lambda i, f: (0, f)),
        pl.BlockSpec((CONV_WIDTH, FC), lambda i, f: (0, nf + f)),
        pl.BlockSpec((1, FC), lambda i, f: (0, f)),
        pl.BlockSpec((1, FC), lambda i, f: (0, nf + f)),
        pl.BlockSpec((FC, D_MODEL), lambda i, f: (f, 0)),
    ]
    return pl.pallas_call(
        functools.partial(_ffn_kernel, tiles_per_seq=tiles_per_seq),
        grid=(tokens // tm, nf),
        in_specs=in_specs,
        out_specs=pl.BlockSpec((tm, D_MODEL), lambda i, f: (i, 0)),
        out_shape=jax.ShapeDtypeStruct(x2.shape, jnp.float32),
        scratch_shapes=[pltpu.VMEM((tm + 2 * HALO, D_MODEL), jnp.bfloat16),
                        pltpu.VMEM((FC // FC_SUB, 2, FC_SUB // LANES, tm + 2 * HALO, LANES), jnp.float32)],
        compiler_params=_params("parallel", "arbitrary"),
        name="ffn",
    )(x2, x2, x2, norm_g, w_up, w_up, conv_w, conv_w, conv_b, conv_b, w_down)


def kernel(x, w_in, b_gate, qk_gain, rel_pos_bias, w_branch, w_out, norm_mix, norm_ffn, w_up, conv_w,
           conv_b, w_down):
    batch, seq, d_model = x.shape
    depth = w_in.shape[0]
    assert d_model == D_MODEL and seq % TM_FFN == 0 and seq % A_TILE == 0
    bf16 = jnp.bfloat16
    x2 = x.reshape(batch * seq, d_model)

    col_order = _qkv_column_order()
    gain_idx = _pair_lane_gain_index()
    mavg = jnp.asarray(_head_mean_matrix(), bf16)
    ropes = _rope_tables(seq)
    log_mult = jnp.asarray(_dilated_log_multiplicity(seq))
    b_rows = A_WIDTH + _b_out_row_order()

    for l in range(depth):
        w_qkv = w_in[l][:, col_order].astype(bf16)
        w_gate = w_in[l][:, QKV_WIDTH:].astype(bf16)
        gains = qk_gain[l].reshape(2 * N_BRANCH, HEAD_DIM)[:, gain_idx]
        wbr = w_branch[l]
        wa = wbr[:A_WIDTH].astype(bf16)
        wb = wbr[b_rows].astype(bf16)
        wc = wbr[A_WIDTH + B_Q_WIDTH:].astype(bf16)

        qa, ka, qb, kb, qc, kc, va, vb, vc = _in_projection(
            x2, norm_mix[l][None, :], w_qkv, gains, mavg, ropes, seq)
        oa = _dense_attention(qa, ka, va, seq, bias=log_mult)
        ob = _dense_attention(qb, kb, vb, seq)
        nbr_bias = _nbr_bias_table(rel_pos_bias[l], seq // GRID_W)
        oc = _nbr_attention(qc, kc, vc, nbr_bias, seq)
        x2 = _merge(x2, norm_mix[l][None, :], oa, ob, oc, w_gate, b_gate[l][None, :], wa, wb, wc,
                    w_out[l].astype(bf16))
        x2 = _ffn(x2, norm_ffn[l][None, :], w_up[l].astype(bf16), conv_w[l], conv_b[l][None, :],
                  w_down[l].astype(bf16), seq)
    return x2.reshape(batch, seq, d_model)
```

```python
import functools

import numpy as np
import jax
import jax.numpy as jnp
from jax import lax
from jax.experimental import pallas as pl
from jax.experimental.pallas import tpu as pltpu

D_MODEL = 1024
GRID_W = 64
HEAD_DIM = 64
HALF = HEAD_DIM // 2
A_HEADS = 4
B_Q_HEADS = 8
B_KV_HEADS = 2
C_HEADS = 4
A_WIDTH = A_HEADS * HEAD_DIM
B_Q_WIDTH = B_Q_HEADS * HEAD_DIM
B_KV_WIDTH = B_KV_HEADS * HEAD_DIM
C_WIDTH = C_HEADS * HEAD_DIM
N_BRANCH = 3
QKV_WIDTH = 3 * A_WIDTH + B_Q_WIDTH + 2 * B_KV_WIDTH + 3 * C_WIDTH
A_PATTERNS = ((128, 1), (512, 4), (2048, 16))
A_CLASSES = A_PATTERNS[-1][1]
C_WIN_ROWS = 8
C_WIN_COLS = 16
ROPE_THETA = 10000.0
D_FF = 3 * D_MODEL
CONV_WIDTH = 3
EPS = 1e-6
NEG_INF = -1e30
LOG2_E = float(np.log2(np.e))

LANES = 128
SUBLANES = 8
VMEM_LIMIT = 56 * 1024 * 1024

TM_PROJ = 512
TQ = 256
A_TILE = 256
C_TILE_ROWS = 4
C_KEY_ROWS = 12
TM_FFN = 1024
FC = 1024
FC_SUB = 256
FC_DOWN = 512
HALO = SUBLANES


def _params(*sem):
    return pltpu.CompilerParams(dimension_semantics=sem, vmem_limit_bytes=VMEM_LIMIT)


def _const_spec(shape):
    nd = len(shape)
    return pl.BlockSpec(shape, lambda *_: (0,) * nd, pipeline_mode=pl.Buffered(1))


def _layer_spec(stacked_shape, layer):
    nd = len(stacked_shape) - 1
    return pl.BlockSpec((None,) + tuple(stacked_shape[1:]), lambda *_: (layer,) + (0,) * nd,
                        pipeline_mode=pl.Buffered(1))


def _pair_cols(base, head_a, head_b):
    lane = np.arange(LANES)
    half, slot, i = lane // HEAD_DIM, (lane % HEAD_DIM) // HALF, lane % HALF
    head = np.where(slot == 0, head_a, head_b)
    return base + head * HEAD_DIM + half * HALF + i


def _qkv_column_order():
    o_qa, o_ka, o_va = 0, A_WIDTH, 2 * A_WIDTH
    o_qb = 3 * A_WIDTH
    o_kb = o_qb + B_Q_WIDTH
    o_vb = o_kb + B_KV_WIDTH
    o_qc = o_vb + B_KV_WIDTH
    o_kc, o_vc = o_qc + C_WIDTH, o_qc + 2 * C_WIDTH
    cols = []
    for base in (o_qa, o_ka):
        cols += [_pair_cols(base, 0, 1), _pair_cols(base, 2, 3)]
    group = B_Q_HEADS // B_KV_HEADS
    cols += [_pair_cols(o_qb, j, group + j) for j in range(group)]
    cols += [_pair_cols(o_kb, 0, 1)]
    for base in (o_qc, o_kc):
        cols += [_pair_cols(base, 0, 1), _pair_cols(base, 2, 3)]
    cols += [np.arange(o_va, o_va + A_WIDTH), np.arange(o_vb, o_vb + B_KV_WIDTH),
             np.arange(o_vc, o_vc + C_WIDTH)]
    return np.concatenate(cols)


def _b_out_row_order():
    group = B_Q_HEADS // B_KV_HEADS
    rows = []
    for j in range(group):
        rows += [np.arange(j * HEAD_DIM, (j + 1) * HEAD_DIM),
                 np.arange((group + j) * HEAD_DIM, (group + j + 1) * HEAD_DIM)]
    return np.concatenate(rows)


def _pair_lane_gain_index():
    lane = np.arange(LANES)
    return (lane // HEAD_DIM) * HALF + lane % HALF


def _head_mean_matrix():
    lane = np.arange(LANES)
    slot = (lane % HEAD_DIM) // HALF
    return (slot[:, None] == slot[None, :]).astype(np.float32) / HEAD_DIM


def _log2_count(count):
    return np.where(count > 0, np.log2(np.maximum(count, 1).astype(np.float32)),
                    np.float32(NEG_INF)).astype(np.float32)


def _dilated_near_bias():
    a = np.arange(A_TILE)
    tiles = []
    for idx in range(3):
        d = (1 - idx) * A_TILE + a[:, None] - a[None, :]
        count = np.zeros(d.shape, np.int32)
        for window, dil in A_PATTERNS[:-1]:
            assert window // 2 <= A_TILE
            count += ((d % dil == 0) & (np.abs(d) <= window // 2)).astype(np.int32)
        tiles.append(np.tile(_log2_count(count), (2, 1)))
    return np.stack(tiles)


def _dilated_far_bias(seq):
    window, dil = A_PATTERNS[-1]
    class_len = seq // dil
    assert A_TILE % class_len == 0
    a = np.arange(A_TILE)
    same = (a[:, None] // class_len) == (a[None, :] // class_len)
    near = np.abs(a[:, None] % class_len - a[None, :] % class_len) <= window // (2 * dil)
    return np.tile(_log2_count((same & near).astype(np.int32)), (2, 1))


def _rope_tables(seq):
    t = jnp.arange(seq)

    def angles(pos, dim):
        inv = ROPE_THETA ** (-jnp.arange(0, dim, 2, dtype=jnp.float32) / dim)
        return pos[:, None] * inv[None, :]

    ang_1d = angles(t.astype(jnp.float32), HEAD_DIM)
    ang_2d = jnp.concatenate([angles((t // GRID_W).astype(jnp.float32), HEAD_DIM // 2),
                              angles((t % GRID_W).astype(jnp.float32), HEAD_DIM // 2)], axis=-1)
    lane = np.arange(LANES)
    idx = lane % HALF
    sign = np.where(lane // HEAD_DIM == 0, -1.0, 1.0).astype(np.float32)
    out = []
    for ang in (ang_1d, ang_2d):
        out += [jnp.cos(ang)[:, idx], jnp.sin(ang)[:, idx] * sign]
    return out


def _rms_normed_bf16(x, g):
    ms = jnp.mean(x * x, axis=-1, keepdims=True)
    return (x * lax.rsqrt(ms + EPS) * g).astype(jnp.bfloat16)


def _inproj_kernel(x_ref, g_ref, w_ref, gain_ref, mavg_ref, cos_a_ref, sin_a_ref,
                   cos_b_ref, sin_b_ref,
                   qa_ref, ka_ref, qb_ref, kb_ref, qc_ref, kc_ref, va_ref, vb_ref, vc_ref,
                   qa_cm_ref, ka_cm_ref, va_cm_ref, cm_ref):
    tm = x_ref.shape[0]
    h = _rms_normed_bf16(x_ref[...], g_ref[...])
    y = jnp.dot(h, w_ref[...], preferred_element_type=jnp.float32)
    mavg = mavg_ref[...]
    scale = HEAD_DIM ** -0.5 * LOG2_E
    staged = [0]

    def emit(t, out_ref, cm_out_ref, c):
        lanes = slice(c * LANES, (c + 1) * LANES)
        out_ref[:, lanes] = t.astype(out_ref.dtype)
        if cm_out_ref is not None:
            stage = cm_ref.at[staged[0]]
            staged[0] += 1
            stage[...] = t
            for r in range(A_CLASSES):
                cm_out_ref[r, :, lanes] = stage[pl.ds(r, tm // A_CLASSES, stride=A_CLASSES), :].astype(
                    cm_out_ref.dtype)

    plan = ((qa_ref, qa_cm_ref, 2, (cos_a_ref, sin_a_ref), True),
            (ka_ref, ka_cm_ref, 2, (cos_a_ref, sin_a_ref), False),
            (qb_ref, None, 4, (cos_b_ref, sin_b_ref), True), (kb_ref, None, 1, (cos_b_ref, sin_b_ref), False),
            (qc_ref, None, 2, None, True), (kc_ref, None, 2, None, False))
    chunk = 0
    gain_row = 0
    for out_ref, cm_out_ref, n_chunks, rope, is_q in plan:
        gain = gain_ref[gain_row:gain_row + 1, :]
        for c in range(n_chunks):
            t = y[:, chunk * LANES:(chunk + 1) * LANES]
            ms = jnp.dot((t * t).astype(jnp.bfloat16), mavg, preferred_element_type=jnp.float32)
            t = t * lax.rsqrt(ms + EPS) * gain
            if rope is not None:
                t = t * rope[0][...] + pltpu.roll(t, HEAD_DIM, axis=1) * rope[1][...]
            if is_q:
                t = t * scale
            emit(t, out_ref, cm_out_ref, c)
            chunk += 1
        gain_row += 1
    for out_ref, cm_out_ref, n_chunks in ((va_ref, va_cm_ref, 2), (vb_ref, None, 1), (vc_ref, None, 2)):
        for c in range(n_chunks):
            emit(y[:, chunk * LANES:(chunk + 1) * LANES], out_ref, cm_out_ref, c)
            chunk += 1


def _in_projection(x2, norm_g, w_qkv, gains, mavg, ropes, seq, layer):
    tokens = x2.shape[0]
    tm = TM_PROJ
    tiles_per_seq = seq // tm
    row = lambda width: pl.BlockSpec((tm, width), lambda i: (i, 0))
    rope_spec = pl.BlockSpec((tm, LANES), lambda i: (i % tiles_per_seq, 0))
    widths = (A_WIDTH, A_WIDTH, B_Q_WIDTH, B_KV_WIDTH, C_WIDTH, C_WIDTH, A_WIDTH, B_KV_WIDTH, C_WIDTH)
    cm_shape = (tokens // seq, A_CLASSES, seq // A_CLASSES, A_WIDTH)
    cm_spec = pl.BlockSpec((None, A_CLASSES, tm // A_CLASSES, A_WIDTH),
                           lambda i: (i // tiles_per_seq, 0, i % tiles_per_seq, 0))
    outs = pl.pallas_call(
        _inproj_kernel,
        grid=(tokens // tm,),
        in_specs=[row(D_MODEL), _layer_spec(norm_g.shape, layer), _layer_spec(w_qkv.shape, layer),
                  _layer_spec(gains.shape, layer), _const_spec((LANES, LANES)),
                  rope_spec, rope_spec, rope_spec, rope_spec],
        out_specs=[row(w) for w in widths] + [cm_spec] * 3,
        out_shape=([jax.ShapeDtypeStruct((tokens, w), jnp.bfloat16) for w in widths]
                   + [jax.ShapeDtypeStruct(cm_shape, jnp.bfloat16)] * 3),
        scratch_shapes=[pltpu.VMEM((3 * A_WIDTH // LANES, tm, LANES), jnp.float32)],
        compiler_params=_params("parallel"),
        name="in_projection",
    )(x2, norm_g, w_qkv, gains, mavg, *ropes)
    return outs[:9], [o.reshape(tokens, A_WIDTH) for o in outs[9:]]


def _pipelined_softmax_attention(n_units, score_fn, value_fn, emit_fn):
    ahead = score_fn(0)
    for u in range(n_units):
        buf, m = ahead
        if u + 1 < n_units:
            ahead = score_fn(u + 1)
        p = jnp.exp2(buf[...] - m)
        den = p.sum(axis=-1, keepdims=True)
        pv = jnp.dot(p.astype(jnp.bfloat16), value_fn(u), preferred_element_type=jnp.float32)
        emit_fn(u, pv, den, m)


def _slot_masks(shape):
    lane = lax.broadcasted_iota(jnp.int32, shape, 1)
    return (lane % HEAD_DIM) < HALF, lane < HEAD_DIM


def _nt_dot(a, b):
    return lax.dot_general(a, b, (((1,), (1,)), ((), ())), preferred_element_type=jnp.float32)


def _stack_heads(q, q_head0):
    zero = jnp.zeros_like(q)
    return jnp.concatenate([jnp.where(q_head0, q, zero), jnp.where(q_head0, zero, q)], axis=0)


def _unstack_heads(x, out_head0):
    n = x.shape[0] // 2
    x = jnp.broadcast_to(x, (2 * n, LANES))
    return jnp.where(out_head0, x[:n], x[n:])


def _gqa_attn_kernel(q_ref, k_ref, v_ref, o_ref, s_ref):
    nq, nk = q_ref.shape[1] // LANES, k_ref.shape[1] // LANES
    q_head0, out_head0 = _slot_masks((q_ref.shape[0], LANES))

    def kv_lanes(h):
        kv_chunk = (h // 2) * nk // nq
        return slice(kv_chunk * LANES, (kv_chunk + 1) * LANES)

    def scores(h):
        c, slot = divmod(h, 2)
        q = q_ref[:, c * LANES:(c + 1) * LANES]
        qm = jnp.where(q_head0 if slot == 0 else ~q_head0, q, jnp.zeros_like(q))
        buf = s_ref.at[h % 2]
        buf[...] = _nt_dot(qm, k_ref[:, kv_lanes(h)])
        return buf, buf[...].max(axis=-1, keepdims=True)

    pair = []

    def emit(h, pv, den, m):
        pair.append(pv / den)
        if h % 2 == 1:
            c = h // 2
            o_ref[:, c * LANES:(c + 1) * LANES] = jnp.where(out_head0, *pair).astype(o_ref.dtype)
            pair.clear()

    _pipelined_softmax_attention(2 * nq, scores, lambda h: v_ref[:, kv_lanes(h)], emit)


def _gqa_attention(q, k, v, seq):
    tokens = q.shape[0]
    q_tiles = seq // TQ
    q_spec = pl.BlockSpec((TQ, q.shape[1]), lambda b, i: (b * q_tiles + i, 0))
    kv_spec = pl.BlockSpec((seq, k.shape[1]), lambda b, i: (b, 0))
    return pl.pallas_call(
        _gqa_attn_kernel,
        grid=(tokens // seq, q_tiles),
        in_specs=[q_spec, kv_spec, kv_spec],
        out_specs=q_spec,
        out_shape=jax.ShapeDtypeStruct(q.shape, jnp.bfloat16),
        scratch_shapes=[pltpu.VMEM((2, TQ, seq), jnp.float32)],
        compiler_params=_params("parallel", "parallel"),
        name="gqa_attention",
    )(q, k, v)


def _dilated_attn_kernel(q_ref, k_ref, v_ref, q_cm_ref, k_cm_ref, v_cm_ref, far_ref, near_ref, o_ref,
                         s_ref, acc_ref, max_ref, den_ref):
    seq = q_ref.shape[0]
    n_chunks = q_ref.shape[1] // LANES
    tile = A_TILE
    n_tiles = seq // tile
    class_len = seq // A_CLASSES
    q_head0, out_head0 = _slot_masks((tile, LANES))

    def far_where(u):
        t, c = divmod(u, n_chunks)
        return t, c, slice(t * tile, (t + 1) * tile), slice(c * LANES, (c + 1) * LANES)

    def far_scores(u):
        _, _, rows, lanes = far_where(u)
        buf = s_ref.at[u % 2, :, 0:tile]
        buf[...] = _nt_dot(_stack_heads(q_cm_ref[rows, lanes], q_head0), k_cm_ref[rows, lanes]) + far_ref[...]
        return buf, buf[...].max(axis=-1, keepdims=True)

    def far_values(u):
        _, _, rows, lanes = far_where(u)
        return v_cm_ref[rows, lanes]

    def far_emit(u, pv, den, m):
        t, c, _, _ = far_where(u)
        for ref, stat in ((acc_ref, pv), (max_ref, m), (den_ref, den)):
            stat = _unstack_heads(stat, out_head0)
            for e in range(tile // class_len):
                cls = t * (tile // class_len) + e
                ref[c, pl.ds(cls, class_len, stride=A_CLASSES), :] = stat[e * class_len:(e + 1) * class_len]

    _pipelined_softmax_attention(n_tiles * n_chunks, far_scores, far_values, far_emit)

    def near_where(u):
        i, c = divmod(u, n_chunks)
        return i, c, max(i - 1, 0), min(i + 1, n_tiles - 1), slice(c * LANES, (c + 1) * LANES)

    def near_scores(u):
        i, _, lo, hi, lanes = near_where(u)
        buf = s_ref.at[u % 2, :, 0:(hi - lo + 1) * tile]
        qs = _stack_heads(q_ref[i * tile:(i + 1) * tile, lanes], q_head0)
        for j in range(lo, hi + 1):
            buf[:, (j - lo) * tile:(j - lo + 1) * tile] = (
                _nt_dot(qs, k_ref[j * tile:(j + 1) * tile, lanes]) + near_ref[j - i + 1])
        return buf, buf[...].max(axis=-1, keepdims=True)

    def near_values(u):
        _, _, lo, hi, lanes = near_where(u)
        return v_ref[lo * tile:(hi + 1) * tile, lanes]

    def near_emit(u, pv, den, m):
        i, c, _, _, lanes = near_where(u)
        rows = slice(i * tile, (i + 1) * tile)
        acc, m, den = (_unstack_heads(stat, out_head0) for stat in (pv, m, den))
        far_m = max_ref[c, rows, :]
        top = jnp.maximum(m, far_m)
        w_near, w_far = jnp.exp2(m - top), jnp.exp2(far_m - top)
        out = (acc * w_near + acc_ref[c, rows, :] * w_far) / (den * w_near + den_ref[c, rows, :] * w_far)
        o_ref[rows, lanes] = out.astype(o_ref.dtype)

    _pipelined_softmax_attention(n_tiles * n_chunks, near_scores, near_values, near_emit)


def _dilated_attention(q, k, v, q_cm, k_cm, v_cm, far_bias, near_bias, seq):
    tokens, width = q.shape
    seq_spec = pl.BlockSpec((seq, width), lambda b: (b, 0))
    stat = pltpu.VMEM((width // LANES, seq, LANES), jnp.float32)
    return pl.pallas_call(
        _dilated_attn_kernel,
        grid=(tokens // seq,),
        in_specs=[seq_spec] * 6 + [_const_spec(far_bias.shape), _const_spec(near_bias.shape)],
        out_specs=seq_spec,
        out_shape=jax.ShapeDtypeStruct(q.shape, jnp.bfloat16),
        scratch_shapes=[pltpu.VMEM((2, 2 * A_TILE, 3 * A_TILE), jnp.float32), stat, stat, stat],
        compiler_params=_params("parallel"),
        name="dilated_attention",
    )(q, k, v, q_cm, k_cm, v_cm, far_bias, near_bias)


def _c_geometry(rows_total):
    n_tiles = rows_total // C_TILE_ROWS
    wr = min(C_WIN_ROWS, rows_total)
    key_units = C_KEY_ROWS // C_TILE_ROWS
    return n_tiles, wr, key_units


def _nbr_bias_kernel(rpb_ref, o_ref, *, rows_total, layer):
    n_tiles, wr, key_units = _c_geometry(rows_total)
    h = pl.program_id(0) + layer * pl.num_programs(0)
    shape = (GRID_W, LANES)
    c = lax.broadcasted_iota(jnp.int32, shape, 0)
    lane = lax.broadcasted_iota(jnp.int32, shape, 1)
    kc = lane % GRID_W
    d = kc - c
    start = jnp.clip(c - C_WIN_COLS // 2, 0, GRID_W - C_WIN_COLS)
    in_window = (kc >= start) & (kc < start + C_WIN_COLS)
    neg = jnp.full(shape, NEG_INF, jnp.float32)
    n_dr = 2 * C_WIN_ROWS - 1
    n_dc = 2 * C_WIN_COLS - 1
    col_bias = []
    for dri in range(n_dr):
        t = neg
        for j in range(n_dc):
            t = jnp.where(d == j - (C_WIN_COLS - 1), rpb_ref[h * n_dr + dri, j], t)
        col_bias.append(jnp.where(in_window, t * LOG2_E, neg))
    left = lane < GRID_W
    for case, tile in enumerate((0, 1, n_tiles - 1)):
        key_row0 = C_TILE_ROWS * min(max(tile - 1, 0), n_tiles - key_units)
        for rq in range(C_TILE_ROWS):
            r = tile * C_TILE_ROWS + rq
            first = min(max(r - wr // 2, 0), rows_total - wr)
            for pair in range(C_KEY_ROWS // 2):
                halves = []
                for e in range(2):
                    kr = key_row0 + 2 * pair + e
                    halves.append(col_bias[kr - r + C_WIN_ROWS - 1] if first <= kr < first + wr else neg)
                o_ref[case, 0, rq * GRID_W:(rq + 1) * GRID_W, pair * LANES:(pair + 1) * LANES] = (
                    jnp.where(left, halves[0], halves[1]))


def _nbr_bias_table(rpb, rows_total, layer):
    depth, heads, n_dr, n_dc = rpb.shape
    tq = C_TILE_ROWS * GRID_W
    tk = C_KEY_ROWS * GRID_W
    return pl.pallas_call(
        functools.partial(_nbr_bias_kernel, rows_total=rows_total, layer=layer),
        grid=(heads,),
        in_specs=[pl.BlockSpec(memory_space=pltpu.SMEM)],
        out_specs=pl.BlockSpec((3, 1, tq, tk), lambda h: (0, h, 0, 0)),
        out_shape=jax.ShapeDtypeStruct((3, heads, tq, tk), jnp.float32),
        compiler_params=_params("parallel"),
        name="nbr_bias_table",
    )(rpb.reshape(depth * heads * n_dr, n_dc))


def _nbr_attn_kernel(q_ref, k_ref, v_ref, bias_ref, o_ref, s_ref, *, rows_total):
    n_tiles, _, key_units = _c_geometry(rows_total)
    tq = C_TILE_ROWS * GRID_W
    n_heads = 2 * (q_ref.shape[1] // LANES)
    q_head0, out_head0 = _slot_masks((tq, LANES))

    def where(u):
        tile, head = divmod(u, n_heads)
        key0 = tq * min(max(tile - 1, 0), n_tiles - key_units)
        case = 0 if tile == 0 else (2 if tile == n_tiles - 1 else 1)
        lanes = slice((head // 2) * LANES, (head // 2 + 1) * LANES)
        return tile, head, slice(key0, key0 + key_units * tq), case, lanes

    def scores(u):
        tile, head, keys, case, lanes = where(u)
        q = q_ref[tile * tq:(tile + 1) * tq, lanes]
        qm = jnp.where(q_head0 if head % 2 == 0 else ~q_head0, q, jnp.zeros_like(q))
        buf = s_ref.at[u % 2]
        buf[...] = _nt_dot(qm, k_ref[keys, lanes]) + bias_ref[case, head]
        return buf, buf[...].max(axis=-1, keepdims=True)

    def values(u):
        _, _, keys, _, lanes = where(u)
        return v_ref[keys, lanes]

    pair = []

    def emit(u, pv, den, m):
        tile, head, _, _, lanes = where(u)
        pair.append(pv / den)
        if head % 2 == 1:
            o_ref[tile * tq:(tile + 1) * tq, lanes] = jnp.where(out_head0, *pair).astype(o_ref.dtype)
            pair.clear()

    _pipelined_softmax_attention(n_tiles * n_heads, scores, values, emit)


def _nbr_attention(q, k, v, bias, seq):
    tokens, width = q.shape
    rows_total = seq // GRID_W
    _, _, key_units = _c_geometry(rows_total)
    tq = C_TILE_ROWS * GRID_W
    seq_spec = pl.BlockSpec((seq, width), lambda b: (b, 0))
    return pl.pallas_call(
        functools.partial(_nbr_attn_kernel, rows_total=rows_total),
        grid=(tokens // seq,),
        in_specs=[seq_spec, seq_spec, seq_spec, _const_spec(bias.shape)],
        out_specs=seq_spec,
        out_shape=jax.ShapeDtypeStruct(q.shape, jnp.bfloat16),
        scratch_shapes=[pltpu.VMEM((2, tq, key_units * tq), jnp.float32)],
        compiler_params=_params("parallel"),
        name="nbr_attention",
    )(q, k, v, bias)


def _merge_kernel(x_ref, g_ref, oa_ref, ob_ref, oc_ref, wg_ref, bg_ref, wbr_ref, wo_ref, o_ref):
    x = x_ref[...]
    h = _rms_normed_bf16(x, g_ref[...])
    merged = None
    row = 0
    for n, br_ref in enumerate((oa_ref, ob_ref, oc_ref)):
        cols = slice(n * D_MODEL, (n + 1) * D_MODEL)
        logits = jnp.dot(h, wg_ref[:, cols], preferred_element_type=jnp.float32) + bg_ref[:, cols]
        gate = 1.0 / (1.0 + jnp.exp(-logits))
        width = br_ref.shape[1]
        y = jnp.dot(br_ref[...], wbr_ref[row:row + width, :], preferred_element_type=jnp.float32)
        row += width
        merged = gate * y if merged is None else merged + gate * y
    o_ref[...] = x + jnp.dot(merged.astype(jnp.bfloat16), wo_ref[...], preferred_element_type=jnp.float32)


def _merge(x2, norm_g, oa, ob, oc, w_gate, b_gate, w_branch, w_out, layer):
    tokens = x2.shape[0]
    tm = TM_PROJ
    row = lambda width: pl.BlockSpec((tm, width), lambda i: (i, 0))
    return pl.pallas_call(
        _merge_kernel,
        grid=(tokens // tm,),
        in_specs=[row(D_MODEL), _layer_spec(norm_g.shape, layer), row(oa.shape[1]), row(ob.shape[1]),
                  row(oc.shape[1]), _layer_spec(w_gate.shape, layer), _layer_spec(b_gate.shape, layer),
                  _layer_spec(w_branch.shape, layer), _layer_spec(w_out.shape, layer)],
        out_specs=row(D_MODEL),
        out_shape=jax.ShapeDtypeStruct(x2.shape, jnp.float32),
        compiler_params=_params("parallel"),
        name="gated_merge",
    )(x2, norm_g, oa, ob, oc, w_gate, b_gate, w_branch, w_out)


def _gelu_tanh(x):
    return 0.5 * x * (1.0 + jnp.tanh(np.sqrt(2.0 / np.pi).astype(np.float32) * (x + 0.044715 * (x * x * x))))


def _ffn_kernel(x_ref, prev_ref, next_ref, g_ref, wg_ref, wv_ref, cwg_ref, cwv_ref, cbg_ref, cbv_ref,
                wd_ref, o_ref, h_ref, u_ref, *, tiles_per_seq):
    i = pl.program_id(0)
    f = pl.program_id(1)
    tm = x_ref.shape[0]

    @pl.when(f == 0)
    def _():
        g = g_ref[...]
        has_prev = (i % tiles_per_seq != 0).astype(jnp.float32)
        has_next = (i % tiles_per_seq != tiles_per_seq - 1).astype(jnp.float32)
        h_ref[0:HALO, :] = _rms_normed_bf16(prev_ref[...] * has_prev, g)
        h_ref[HALO:HALO + tm, :] = _rms_normed_bf16(x_ref[...], g)
        h_ref[HALO + tm:, :] = _rms_normed_bf16(next_ref[...] * has_next, g)
        o_ref[...] = x_ref[...]

    h = h_ref[...]

    def conv(w_ref, cw_ref, cb_ref, start, u_scr):
        u = jnp.dot(h, w_ref[:, start:start + FC_SUB], preferred_element_type=jnp.float32)
        outs = []
        for ct in range(FC_SUB // LANES):
            u_scr[ct] = u[:, ct * LANES:(ct + 1) * LANES]
            cols = slice(start + ct * LANES, start + (ct + 1) * LANES)
            out = cb_ref[:, cols] + u_scr[ct, pl.ds(HALO - 1, tm), :] * cw_ref[0:1, cols]
            for j in range(1, CONV_WIDTH):
                out = out + u_scr[ct, pl.ds(HALO - 1 + j, tm), :] * cw_ref[j:j + 1, cols]
            outs.append(out)
        return jnp.concatenate(outs, axis=1)

    for pair in range(FC // FC_DOWN):
        acts = []
        for s in range(FC_DOWN // FC_SUB):
            n = pair * (FC_DOWN // FC_SUB) + s
            start = n * FC_SUB
            act = (_gelu_tanh(conv(wg_ref, cwg_ref, cbg_ref, start, u_ref.at[n, 0]))
                   * conv(wv_ref, cwv_ref, cbv_ref, start, u_ref.at[n, 1]))
            acts.append(act.astype(jnp.bfloat16))
        rows = slice(pair * FC_DOWN, (pair + 1) * FC_DOWN)
        o_ref[...] += jnp.dot(jnp.concatenate(acts, axis=1), wd_ref[rows, :],
                              preferred_element_type=jnp.float32)


def _ffn(x2, norm_g, w_up, conv_w, conv_b, w_down, seq, layer):
    tokens = x2.shape[0]
    tm = TM_FFN
    tiles_per_seq = seq // tm
    nf = D_FF // FC
    halo_blocks = tm // HALO
    last_halo = tokens // HALO - 1
    in_specs = [
        pl.BlockSpec((tm, D_MODEL), lambda i, f: (i, 0)),
        pl.BlockSpec((HALO, D_MODEL), lambda i, f: (jnp.maximum(i * halo_blocks - 1, 0), 0)),
        pl.BlockSpec((HALO, D_MODEL), lambda i, f: (jnp.minimum((i + 1) * halo_blocks, last_halo), 0)),
        pl.BlockSpec((None, 1, D_MODEL), lambda i, f: (layer, 0, 0)),
        pl.BlockSpec((None, D_MODEL, FC), lambda i, f: (layer, 0, f)),
        pl.BlockSpec((None, D_MODEL, FC), lambda i, f: (layer, 0, nf + f)),
        pl.BlockSpec((None, CONV_WIDTH, FC), lambda i, f: (layer, 0, f)),
        pl.BlockSpec((None, CONV_WIDTH, FC), lambda i, f: (layer, 0, nf + f)),
        pl.BlockSpec((None, 1, FC), lambda i, f: (layer, 0, f)),
        pl.BlockSpec((None, 1, FC), lambda i, f: (layer, 0, nf + f)),
        pl.BlockSpec((None, FC, D_MODEL), lambda i, f: (layer, f, 0)),
    ]
    return pl.pallas_call(
        functools.partial(_ffn_kernel, tiles_per_seq=tiles_per_seq),
        grid=(tokens // tm, nf),
        in_specs=in_specs,
        out_specs=pl.BlockSpec((tm, D_MODEL), lambda i, f: (i, 0)),
        out_shape=jax.ShapeDtypeStruct(x2.shape, jnp.float32),
        scratch_shapes=[pltpu.VMEM((tm + 2 * HALO, D_MODEL), jnp.bfloat16),
                        pltpu.VMEM((FC // FC_SUB, 2, FC_SUB // LANES, tm + 2 * HALO, LANES), jnp.float32)],
        compiler_params=_params("parallel", "arbitrary"),
        name="ffn",
    )(x2, x2, x2, norm_g, w_up, w_up, conv_w, conv_w, conv_b, conv_b, w_down)


def kernel(x, w_in, b_gate, qk_gain, rel_pos_bias, w_branch, w_out, norm_mix, norm_ffn, w_up, conv_w,
           conv_b, w_down):
    batch, seq, d_model = x.shape
    depth = w_in.shape[0]
    assert d_model == D_MODEL and seq % TM_FFN == 0 and seq % A_TILE == 0 and seq % (A_CLASSES * 16) == 0
    bf16 = jnp.bfloat16
    x2 = x.reshape(batch * seq, d_model)

    w_qkv = w_in[:, :, _qkv_column_order()].astype(bf16)
    w_gate = w_in[:, :, QKV_WIDTH:].astype(bf16)
    gains = qk_gain.reshape(depth, 2 * N_BRANCH, HEAD_DIM)[:, :, _pair_lane_gain_index()]
    b_rows = np.concatenate([np.arange(A_WIDTH), A_WIDTH + _b_out_row_order(),
                             np.arange(A_WIDTH + B_Q_WIDTH, D_MODEL)])
    w_br = w_branch[:, b_rows, :].astype(bf16)
    w_out_bf, w_up_bf, w_down_bf = w_out.astype(bf16), w_up.astype(bf16), w_down.astype(bf16)
    norm_mix3, norm_ffn3 = norm_mix[:, None, :], norm_ffn[:, None, :]
    b_gate3, conv_b3 = b_gate[:, None, :], conv_b[:, None, :]
    mavg = jnp.asarray(_head_mean_matrix(), bf16)
    ropes = _rope_tables(seq)
    far_bias = jnp.asarray(_dilated_far_bias(seq))
    near_bias = jnp.asarray(_dilated_near_bias())

    for l in range(depth):
        (qa, ka, qb, kb, qc, kc, va, vb, vc), (qa_cm, ka_cm, va_cm) = _in_projection(
            x2, norm_mix3, w_qkv, gains, mavg, ropes, seq, l)
        oa = _dilated_attention(qa, ka, va, qa_cm, ka_cm, va_cm, far_bias, near_bias, seq)
        ob = _gqa_attention(qb, kb, vb, seq)
        nbr_bias = _nbr_bias_table(rel_pos_bias, seq // GRID_W, l)
        oc = _nbr_attention(qc, kc, vc, nbr_bias, seq)
        x2 = _merge(x2, norm_mix3, oa, ob, oc, w_gate, b_gate3, w_br, w_out_bf, l)
        x2 = _ffn(x2, norm_ffn3, w_up_bf, conv_w, conv_b3, w_down_bf, seq, l)
    return x2.reshape(batch, seq, d_model)
```

```python
import functools

import numpy as np
import jax
import jax.numpy as jnp
from jax import lax
from jax.experimental import pallas as pl
from jax.experimental.pallas import tpu as pltpu

D_MODEL = 1024
GRID_W = 64
HEAD_DIM = 64
HALF = HEAD_DIM // 2
A_HEADS = 4
B_Q_HEADS = 8
B_KV_HEADS = 2
C_HEADS = 4
A_WIDTH = A_HEADS * HEAD_DIM
B_Q_WIDTH = B_Q_HEADS * HEAD_DIM
B_KV_WIDTH = B_KV_HEADS * HEAD_DIM
C_WIDTH = C_HEADS * HEAD_DIM
N_BRANCH = 3
QKV_WIDTH = 3 * A_WIDTH + B_Q_WIDTH + 2 * B_KV_WIDTH + 3 * C_WIDTH
A_PATTERNS = ((128, 1), (512, 4), (2048, 16))
A_CLASSES = A_PATTERNS[-1][1]
C_WIN_ROWS = 8
C_WIN_COLS = 16
ROPE_THETA = 10000.0
D_FF = 3 * D_MODEL
CONV_WIDTH = 3
EPS = 1e-6
NEG_INF = -1e30
LOG2_E = float(np.log2(np.e))

LANES = 128
SUBLANES = 8
VMEM_LIMIT = 56 * 1024 * 1024

TM_PROJ = 512
TQ = 256
GQA_KEY_SPLITS = 2
GQA_STEP_ROWS = 512
A_TILE = 256
C_TILE_ROWS = 4
C_KEY_ROWS = 12
TM_FFN = 1024
FC = 1024
FC_SUB = 256
FC_DOWN = 512
HALO = SUBLANES


def _params(*sem):
    return pltpu.CompilerParams(dimension_semantics=sem, vmem_limit_bytes=VMEM_LIMIT)


def _const_spec(shape):
    nd = len(shape)
    return pl.BlockSpec(shape, lambda *_: (0,) * nd, pipeline_mode=pl.Buffered(1))


def _layer_spec(stacked_shape, layer):
    nd = len(stacked_shape) - 1
    return pl.BlockSpec((None,) + tuple(stacked_shape[1:]), lambda *_: (layer,) + (0,) * nd,
                        pipeline_mode=pl.Buffered(1))


def _head_mean_matrix():
    head = np.arange(2 * LANES) // HEAD_DIM
    return (head[:, None] == head[None, :]).astype(np.float32) / HEAD_DIM


def _log2_count(count):
    return np.where(count > 0, np.log2(np.maximum(count, 1).astype(np.float32)),
                    np.float32(NEG_INF)).astype(np.float32)


def _dilated_near_bias():
    a = np.arange(A_TILE)
    tiles = []
    for idx in range(3):
        d = (1 - idx) * A_TILE + a[:, None] - a[None, :]
        count = np.zeros(d.shape, np.int32)
        for window, dil in A_PATTERNS[:-1]:
            assert window // 2 <= A_TILE
            count += ((d % dil == 0) & (np.abs(d) <= window // 2)).astype(np.int32)
        tiles.append(np.tile(_log2_count(count), (2, 1)))
    return np.stack(tiles)


def _dilated_far_bias(seq):
    window, dil = A_PATTERNS[-1]
    class_len = seq // dil
    assert A_TILE % class_len == 0
    a = np.arange(A_TILE)
    same = (a[:, None] // class_len) == (a[None, :] // class_len)
    near = np.abs(a[:, None] % class_len - a[None, :] % class_len) <= window // (2 * dil)
    return np.tile(_log2_count((same & near).astype(np.int32)), (2, 1))


def _rope_tables(seq):
    t = jnp.arange(seq)

    def angles(pos, dim):
        inv = ROPE_THETA ** (-jnp.arange(0, dim, 2, dtype=jnp.float32) / dim)
        return pos[:, None] * inv[None, :]

    ang_1d = angles(t.astype(jnp.float32), HEAD_DIM)
    ang_2d = jnp.concatenate([angles((t // GRID_W).astype(jnp.float32), HEAD_DIM // 2),
                              angles((t % GRID_W).astype(jnp.float32), HEAD_DIM // 2)], axis=-1)
    lane = np.arange(LANES)
    idx = lane % HALF
    sign = np.where(lane % HEAD_DIM < HALF, -1.0, 1.0).astype(np.float32)
    out = []
    for ang in (ang_1d, ang_2d):
        out += [jnp.cos(ang)[:, idx], jnp.sin(ang)[:, idx] * sign]
    return out


def _rms_normed_bf16(x, g):
    ms = jnp.mean(x * x, axis=-1, keepdims=True)
    return (x * lax.rsqrt(ms + EPS) * g).astype(jnp.bfloat16)


def _inproj_kernel(x_ref, g_ref, w_ref, gain_ref, mavg_ref, cos_a_ref, sin_a_ref,
                   cos_b_ref, sin_b_ref,
                   qa_ref, ka_ref, va_ref, qb_ref, kb_ref, vbt_ref, qc_ref, kc_ref, vc_ref,
                   qa_cm_ref, ka_cm_ref, va_cm_ref, cm_ref):
    tm = x_ref.shape[0]
    h = _rms_normed_bf16(x_ref[...], g_ref[...])
    scale = HEAD_DIM ** -0.5 * LOG2_E
    lane = lax.broadcasted_iota(jnp.int32, (tm, LANES), 1)
    head0 = lane < HEAD_DIM
    first_half = lane % HEAD_DIM < HALF
    staged = [0]

    def project(start, width):
        return jnp.dot(h, w_ref[:, start:start + width], preferred_element_type=jnp.float32)

    def normed(y, gain_row, rope, is_q):
        width = y.shape[1]
        ms = jnp.dot((y * y).astype(jnp.bfloat16), mavg_ref[0:width, 0:width],
                     preferred_element_type=jnp.float32)
        y = y * lax.rsqrt(ms + EPS)
        gain = gain_ref[gain_row:gain_row + 1, :]
        chunks = []
        for c in range(width // LANES):
            t = y[:, c * LANES:(c + 1) * LANES] * gain
            if rope is not None:
                partner = jnp.where(first_half, pltpu.roll(t, LANES - HALF, axis=1), pltpu.roll(t, HALF, axis=1))
                t = t * rope[0][...] + partner * rope[1][...]
            chunks.append(t * scale if is_q else t)
        return chunks

    def plain(y):
        return [y[:, c * LANES:(c + 1) * LANES] for c in range(y.shape[1] // LANES)]

    def emit(chunks, out_ref, cm_out_ref=None):
        for c, t in enumerate(chunks):
            lanes = slice(c * LANES, (c + 1) * LANES)
            out_ref[:, lanes] = t.astype(out_ref.dtype)
            if cm_out_ref is not None:
                stage = cm_ref.at[staged[0]]
                staged[0] += 1
                stage[...] = t
                for r in range(A_CLASSES):
                    cm_out_ref[r, :, lanes] = stage[pl.ds(r, tm // A_CLASSES, stride=A_CLASSES), :].astype(
                        cm_out_ref.dtype)

    rope_a, rope_b = (cos_a_ref, sin_a_ref), (cos_b_ref, sin_b_ref)
    y = project(0, 3 * A_WIDTH)
    emit(normed(y[:, 0:A_WIDTH], 0, rope_a, True), qa_ref, qa_cm_ref)
    emit(normed(y[:, A_WIDTH:2 * A_WIDTH], 1, rope_a, False), ka_ref, ka_cm_ref)
    emit(plain(y[:, 2 * A_WIDTH:]), va_ref, va_cm_ref)

    y = project(3 * A_WIDTH, B_Q_WIDTH + 2 * B_KV_WIDTH)
    pair = 2 * LANES
    emit(normed(y[:, 0:pair], 2, rope_b, True) + normed(y[:, pair:B_Q_WIDTH], 2, rope_b, True), qb_ref)
    (kb,) = normed(y[:, B_Q_WIDTH:B_Q_WIDTH + B_KV_WIDTH], 3, rope_b, False)
    swapped = pltpu.roll(kb, HEAD_DIM, axis=1)
    emit([jnp.where(head0, kb, swapped), jnp.where(head0, swapped, kb)], kb_ref)
    vbt_ref[...] = y[:, B_Q_WIDTH + B_KV_WIDTH:].T.astype(vbt_ref.dtype)

    y = project(3 * A_WIDTH + B_Q_WIDTH + 2 * B_KV_WIDTH, 3 * C_WIDTH)
    emit(normed(y[:, 0:C_WIDTH], 4, None, True), qc_ref)
    emit(normed(y[:, C_WIDTH:2 * C_WIDTH], 5, None, False), kc_ref)
    emit(plain(y[:, 2 * C_WIDTH:]), vc_ref)


def _in_projection(x2, norm_g, w_qkv, gains, mavg, ropes, seq, layer):
    tokens = x2.shape[0]
    tm = TM_PROJ
    tiles_per_seq = seq // tm
    row = lambda width: pl.BlockSpec((tm, width), lambda i: (i, 0))
    rope_spec = pl.BlockSpec((tm, LANES), lambda i: (i % tiles_per_seq, 0))
    bf16 = jnp.bfloat16
    cm_shape = (tokens // seq, A_CLASSES, seq // A_CLASSES, A_WIDTH)
    cm_spec = pl.BlockSpec((None, A_CLASSES, tm // A_CLASSES, A_WIDTH),
                           lambda i: (i // tiles_per_seq, 0, i % tiles_per_seq, 0))
    outs = [((tokens, A_WIDTH), row(A_WIDTH))] * 3
    outs += [((tokens, B_Q_WIDTH), row(B_Q_WIDTH)), ((tokens, 2 * B_KV_WIDTH), row(2 * B_KV_WIDTH)),
             ((B_KV_WIDTH, tokens), pl.BlockSpec((B_KV_WIDTH, tm), lambda i: (0, i)))]
    outs += [((tokens, C_WIDTH), row(C_WIDTH))] * 3
    outs += [(cm_shape, cm_spec)] * 3
    res = pl.pallas_call(
        _inproj_kernel,
        grid=(tokens // tm,),
        in_specs=[row(D_MODEL), _layer_spec(norm_g.shape, layer), _layer_spec(w_qkv.shape, layer),
                  _layer_spec(gains.shape, layer), _const_spec(mavg.shape),
                  rope_spec, rope_spec, rope_spec, rope_spec],
        out_specs=[spec for _, spec in outs],
        out_shape=[jax.ShapeDtypeStruct(shape, bf16) for shape, _ in outs],
        scratch_shapes=[pltpu.VMEM((3 * A_WIDTH // LANES, tm, LANES), jnp.float32)],
        compiler_params=_params("parallel"),
        name="in_projection",
    )(x2, norm_g, w_qkv, gains, mavg, *ropes)
    return res[:9], [o.reshape(tokens, A_WIDTH) for o in res[9:]]


def _pipelined_softmax_attention(n_units, score_fn, value_fn, emit_fn):
    ahead = score_fn(0)
    for u in range(n_units):
        buf, m = ahead
        if u + 1 < n_units:
            ahead = score_fn(u + 1)
        p = jnp.exp2(buf[...] - m)
        den = p.sum(axis=-1, keepdims=True)
        pv = jnp.dot(p.astype(jnp.bfloat16), value_fn(u), preferred_element_type=jnp.float32)
        emit_fn(u, pv, den, m)


def _pipelined_softmax_attention_t(n_units, score_fn, pv_fn, emit_fn, key_splits):
    ahead = score_fn(0)
    for u in range(n_units):
        buf, m = ahead
        if u + 1 < n_units:
            ahead = score_fn(u + 1)
        step = buf.shape[0] // key_splits
        acc = den = None
        for s in range(key_splits):
            rows = slice(s * step, (s + 1) * step)
            p = jnp.exp2(buf[rows, :] - m)
            part_den = p.sum(axis=0, keepdims=True)
            part = pv_fn(u, rows, p.astype(jnp.bfloat16))
            acc = part if acc is None else acc + part
            den = part_den if den is None else den + part_den
        emit_fn(u, acc, den)


def _head0_lanes(shape):
    return lax.broadcasted_iota(jnp.int32, shape, 1) < HEAD_DIM


def _nt_dot(a, b):
    return lax.dot_general(a, b, (((1,), (1,)), ((), ())), preferred_element_type=jnp.float32)


def _stack_heads(q, head0):
    zero = jnp.zeros_like(q)
    return jnp.concatenate([jnp.where(head0, q, zero), jnp.where(head0, zero, q)], axis=0)


def _unstack_heads(x, head0):
    n = x.shape[0] // 2
    x = jnp.broadcast_to(x, (2 * n, LANES))
    return jnp.where(head0, x[:n], x[n:])


def _gqa_attn_kernel(q_ref, k_ref, vt_ref, o_ref, s_ref):
    nq, groups = q_ref.shape[1] // LANES, k_ref.shape[1] // LANES
    head0 = _head0_lanes((TQ, LANES))

    def where(u):
        tile, c = divmod(u, nq)
        return slice(tile * TQ, (tile + 1) * TQ), slice(c * LANES, (c + 1) * LANES), c * groups // nq

    def scores(u):
        rows, lanes, g = where(u)
        qs = _stack_heads(q_ref[rows, lanes], head0)
        buf = s_ref.at[u % 2]
        buf[...] = _nt_dot(k_ref[:, g * LANES:(g + 1) * LANES], qs)
        return buf, buf[...].max(axis=0, keepdims=True)

    def pv(u, keys, p):
        _, _, g = where(u)
        return jnp.dot(vt_ref[g * HEAD_DIM:(g + 1) * HEAD_DIM, keys], p, preferred_element_type=jnp.float32)

    def emit(u, acc, den):
        rows, lanes, _ = where(u)
        o_t = acc / den
        both = jnp.concatenate([o_t[:, :TQ], o_t[:, TQ:]], axis=0)
        o_ref[rows, lanes] = both.T.astype(o_ref.dtype)

    _pipelined_softmax_attention_t((q_ref.shape[0] // TQ) * nq, scores, pv, emit, GQA_KEY_SPLITS)


def _gqa_attention(q, k2, vt, seq):
    tokens = q.shape[0]
    steps = seq // GQA_STEP_ROWS
    q_spec = pl.BlockSpec((GQA_STEP_ROWS, q.shape[1]), lambda b, i: (b * steps + i, 0))
    return pl.pallas_call(
        _gqa_attn_kernel,
        grid=(tokens // seq, steps),
        in_specs=[q_spec, pl.BlockSpec((seq, k2.shape[1]), lambda b, i: (b, 0)),
                  pl.BlockSpec((vt.shape[0], seq), lambda b, i: (0, b))],
        out_specs=q_spec,
        out_shape=jax.ShapeDtypeStruct(q.shape, jnp.bfloat16),
        scratch_shapes=[pltpu.VMEM((2, seq, 2 * TQ), jnp.float32)],
        compiler_params=_params("parallel", "parallel"),
        name="gqa_attention",
    )(q, k2, vt)


def _dilated_attn_kernel(q_ref, k_ref, v_ref, q_cm_ref, k_cm_ref, v_cm_ref, far_ref, near_ref, o_ref,
                         s_ref, acc_ref, max_ref, den_ref):
    seq = q_ref.shape[0]
    n_chunks = q_ref.shape[1] // LANES
    tile = A_TILE
    n_tiles = seq // tile
    class_len = seq // A_CLASSES
    head0 = _head0_lanes((tile, LANES))

    def far_where(u):
        t, c = divmod(u, n_chunks)
        return t, c, slice(t * tile, (t + 1) * tile), slice(c * LANES, (c + 1) * LANES)

    def far_scores(u):
        _, _, rows, lanes = far_where(u)
        buf = s_ref.at[u % 2, :, 0:tile]
        buf[...] = _nt_dot(_stack_heads(q_cm_ref[rows, lanes], head0), k_cm_ref[rows, lanes]) + far_ref[...]
        return buf, buf[...].max(axis=-1, keepdims=True)

    def far_values(u):
        _, _, rows, lanes = far_where(u)
        return v_cm_ref[rows, lanes]

    def far_emit(u, pv, den, m):
        t, c, _, _ = far_where(u)
        for ref, stat in ((acc_ref, pv), (max_ref, m), (den_ref, den)):
            stat = _unstack_heads(stat, head0)
            for e in range(tile // class_len):
                cls = t * (tile // class_len) + e
                ref[c, pl.ds(cls, class_len, stride=A_CLASSES), :] = stat[e * class_len:(e + 1) * class_len]

    _pipelined_softmax_attention(n_tiles * n_chunks, far_scores, far_values, far_emit)

    def near_where(u):
        i, c = divmod(u, n_chunks)
        return i, c, max(i - 1, 0), min(i + 1, n_tiles - 1), slice(c * LANES, (c + 1) * LANES)

    def near_scores(u):
        i, _, lo, hi, lanes = near_where(u)
        buf = s_ref.at[u % 2, :, 0:(hi - lo + 1) * tile]
        qs = _stack_heads(q_ref[i * tile:(i + 1) * tile, lanes], head0)
        for j in range(lo, hi + 1):
            buf[:, (j - lo) * tile:(j - lo + 1) * tile] = (
                _nt_dot(qs, k_ref[j * tile:(j + 1) * tile, lanes]) + near_ref[j - i + 1])
        return buf, buf[...].max(axis=-1, keepdims=True)

    def near_values(u):
        _, _, lo, hi, lanes = near_where(u)
        return v_ref[lo * tile:(hi + 1) * tile, lanes]

    def near_emit(u, pv, den, m):
        i, c, _, _, lanes = near_where(u)
        rows = slice(i * tile, (i + 1) * tile)
        acc, m, den = (_unstack_heads(stat, head0) for stat in (pv, m, den))
        far_m = max_ref[c, rows, :]
        top = jnp.maximum(m, far_m)
        w_near, w_far = jnp.exp2(m - top), jnp.exp2(far_m - top)
        out = (acc * w_near + acc_ref[c, rows, :] * w_far) / (den * w_near + den_ref[c, rows, :] * w_far)
        o_ref[rows, lanes] = out.astype(o_ref.dtype)

    _pipelined_softmax_attention(n_tiles * n_chunks, near_scores, near_values, near_emit)


def _dilated_attention(q, k, v, q_cm, k_cm, v_cm, far_bias, near_bias, seq):
    tokens, width = q.shape
    seq_spec = pl.BlockSpec((seq, width), lambda b: (b, 0))
    stat = pltpu.VMEM((width // LANES, seq, LANES), jnp.float32)
    return pl.pallas_call(
        _dilated_attn_kernel,
        grid=(tokens // seq,),
        in_specs=[seq_spec] * 6 + [_const_spec(far_bias.shape), _const_spec(near_bias.shape)],
        out_specs=seq_spec,
        out_shape=jax.ShapeDtypeStruct(q.shape, jnp.bfloat16),
        scratch_shapes=[pltpu.VMEM((2, 2 * A_TILE, 3 * A_TILE), jnp.float32), stat, stat, stat],
        compiler_params=_params("parallel"),
        name="dilated_attention",
    )(q, k, v, q_cm, k_cm, v_cm, far_bias, near_bias)


def _c_geometry(rows_total):
    n_tiles = rows_total // C_TILE_ROWS
    wr = min(C_WIN_ROWS, rows_total)
    key_units = C_KEY_ROWS // C_TILE_ROWS
    return n_tiles, wr, key_units


def _nbr_bias_kernel(rpb_ref, o_ref, *, rows_total, layer):
    n_tiles, wr, key_units = _c_geometry(rows_total)
    h = pl.program_id(0) + layer * pl.num_programs(0)
    shape = (GRID_W, LANES)
    c = lax.broadcasted_iota(jnp.int32, shape, 0)
    lane = lax.broadcasted_iota(jnp.int32, shape, 1)
    kc = lane % GRID_W
    d = kc - c
    start = jnp.clip(c - C_WIN_COLS // 2, 0, GRID_W - C_WIN_COLS)
    in_window = (kc >= start) & (kc < start + C_WIN_COLS)
    neg = jnp.full(shape, NEG_INF, jnp.float32)
    n_dr = 2 * C_WIN_ROWS - 1
    n_dc = 2 * C_WIN_COLS - 1
    col_bias = []
    for dri in range(n_dr):
        t = neg
        for j in range(n_dc):
            t = jnp.where(d == j - (C_WIN_COLS - 1), rpb_ref[h * n_dr + dri, j], t)
        col_bias.append(jnp.where(in_window, t * LOG2_E, neg))
    left = lane < GRID_W
    for case, tile in enumerate((0, 1, n_tiles - 1)):
        key_row0 = C_TILE_ROWS * min(max(tile - 1, 0), n_tiles - key_units)
        for rq in range(C_TILE_ROWS):
            r = tile * C_TILE_ROWS + rq
            first = min(max(r - wr // 2, 0), rows_total - wr)
            for pair in range(C_KEY_ROWS // 2):
                halves = []
                for e in range(2):
                    kr = key_row0 + 2 * pair + e
                    halves.append(col_bias[kr - r + C_WIN_ROWS - 1] if first <= kr < first + wr else neg)
                o_ref[case, 0, rq * GRID_W:(rq + 1) * GRID_W, pair * LANES:(pair + 1) * LANES] = (
                    jnp.where(left, halves[0], halves[1]))


def _nbr_bias_table(rpb, rows_total, layer):
    depth, heads, n_dr, n_dc = rpb.shape
    tq = C_TILE_ROWS * GRID_W
    tk = C_KEY_ROWS * GRID_W
    return pl.pallas_call(
        functools.partial(_nbr_bias_kernel, rows_total=rows_total, layer=layer),
        grid=(heads,),
        in_specs=[pl.BlockSpec(memory_space=pltpu.SMEM)],
        out_specs=pl.BlockSpec((3, 1, tq, tk), lambda h: (0, h, 0, 0)),
        out_shape=jax.ShapeDtypeStruct((3, heads, tq, tk), jnp.float32),
        compiler_params=_params("parallel"),
        name="nbr_bias_table",
    )(rpb.reshape(depth * heads * n_dr, n_dc))


def _nbr_attn_kernel(q_ref, k_ref, v_ref, bias_ref, o_ref, s_ref, *, rows_total):
    n_tiles, _, key_units = _c_geometry(rows_total)
    tq = C_TILE_ROWS * GRID_W
    n_heads = 2 * (q_ref.shape[1] // LANES)
    head0 = _head0_lanes((tq, LANES))

    def where(u):
        tile, head = divmod(u, n_heads)
        key0 = tq * min(max(tile - 1, 0), n_tiles - key_units)
        case = 0 if tile == 0 else (2 if tile == n_tiles - 1 else 1)
        lanes = slice((head // 2) * LANES, (head // 2 + 1) * LANES)
        return tile, head, slice(key0, key0 + key_units * tq), case, lanes

    def scores(u):
        tile, head, keys, case, lanes = where(u)
        q = q_ref[tile * tq:(tile + 1) * tq, lanes]
        qm = jnp.where(head0 if head % 2 == 0 else ~head0, q, jnp.zeros_like(q))
        buf = s_ref.at[u % 2]
        buf[...] = _nt_dot(qm, k_ref[keys, lanes]) + bias_ref[case, head]
        return buf, buf[...].max(axis=-1, keepdims=True)

    def values(u):
        _, _, keys, _, lanes = where(u)
        return v_ref[keys, lanes]

    pair = []

    def emit(u, pv, den, m):
        tile, head, _, _, lanes = where(u)
        pair.append(pv / den)
        if head % 2 == 1:
            o_ref[tile * tq:(tile + 1) * tq, lanes] = jnp.where(head0, *pair).astype(o_ref.dtype)
            pair.clear()

    _pipelined_softmax_attention(n_tiles * n_heads, scores, values, emit)


def _nbr_attention(q, k, v, bias, seq):
    tokens, width = q.shape
    rows_total = seq // GRID_W
    _, _, key_units = _c_geometry(rows_total)
    tq = C_TILE_ROWS * GRID_W
    seq_spec = pl.BlockSpec((seq, width), lambda b: (b, 0))
    return pl.pallas_call(
        functools.partial(_nbr_attn_kernel, rows_total=rows_total),
        grid=(tokens // seq,),
        in_specs=[seq_spec, seq_spec, seq_spec, _const_spec(bias.shape)],
        out_specs=seq_spec,
        out_shape=jax.ShapeDtypeStruct(q.shape, jnp.bfloat16),
        scratch_shapes=[pltpu.VMEM((2, tq, key_units * tq), jnp.float32)],
        compiler_params=_params("parallel"),
        name="nbr_attention",
    )(q, k, v, bias)


def _merge_kernel(x_ref, g_ref, oa_ref, ob_ref, oc_ref, wg_ref, bg_ref, wbr_ref, wo_ref, o_ref):
    x = x_ref[...]
    h = _rms_normed_bf16(x, g_ref[...])
    merged = None
    row = 0
    for n, br_ref in enumerate((oa_ref, ob_ref, oc_ref)):
        cols = slice(n * D_MODEL, (n + 1) * D_MODEL)
        logits = jnp.dot(h, wg_ref[:, cols], preferred_element_type=jnp.float32) + bg_ref[:, cols]
        gate = 1.0 / (1.0 + jnp.exp(-logits))
        width = br_ref.shape[1]
        y = jnp.dot(br_ref[...], wbr_ref[row:row + width, :], preferred_element_type=jnp.float32)
        row += width
        merged = gate * y if merged is None else merged + gate * y
    o_ref[...] = x + jnp.dot(merged.astype(jnp.bfloat16), wo_ref[...], preferred_element_type=jnp.float32)


def _merge(x2, norm_g, oa, ob, oc, w_gate, b_gate, w_branch, w_out, layer):
    tokens = x2.shape[0]
    tm = TM_PROJ
    row = lambda width: pl.BlockSpec((tm, width), lambda i: (i, 0))
    return pl.pallas_call(
        _merge_kernel,
        grid=(tokens // tm,),
        in_specs=[row(D_MODEL), _layer_spec(norm_g.shape, layer), row(oa.shape[1]), row(ob.shape[1]),
                  row(oc.shape[1]), _layer_spec(w_gate.shape, layer), _layer_spec(b_gate.shape, layer),
                  _layer_spec(w_branch.shape, layer), _layer_spec(w_out.shape, layer)],
        out_specs=row(D_MODEL),
        out_shape=jax.ShapeDtypeStruct(x2.shape, jnp.float32),
        compiler_params=_params("parallel"),
        name="gated_merge",
    )(x2, norm_g, oa, ob, oc, w_gate, b_gate, w_branch, w_out)


def _gelu_tanh(x):
    return 0.5 * x * (1.0 + jnp.tanh(np.sqrt(2.0 / np.pi).astype(np.float32) * (x + 0.044715 * (x * x * x))))


def _ffn_kernel(x_ref, prev_ref, next_ref, g_ref, wg_ref, wv_ref, cwg_ref, cwv_ref, cbg_ref, cbv_ref,
                wd_ref, o_ref, h_ref, u_ref, *, tiles_per_seq):
    i = pl.program_id(0)
    f = pl.program_id(1)
    tm = x_ref.shape[0]

    @pl.when(f == 0)
    def _():
        g = g_ref[...]
        has_prev = (i % tiles_per_seq != 0).astype(jnp.float32)
        has_next = (i % tiles_per_seq != tiles_per_seq - 1).astype(jnp.float32)
        h_ref[0:HALO, :] = _rms_normed_bf16(prev_ref[...] * has_prev, g)
        h_ref[HALO:HALO + tm, :] = _rms_normed_bf16(x_ref[...], g)
        h_ref[HALO + tm:, :] = _rms_normed_bf16(next_ref[...] * has_next, g)
        o_ref[...] = x_ref[...]

    h = h_ref[...]

    def conv(w_ref, cw_ref, cb_ref, start, u_scr):
        u = jnp.dot(h, w_ref[:, start:start + FC_SUB], preferred_element_type=jnp.float32)
        outs = []
        for ct in range(FC_SUB // LANES):
            u_scr[ct] = u[:, ct * LANES:(ct + 1) * LANES]
            cols = slice(start + ct * LANES, start + (ct + 1) * LANES)
            out = cb_ref[:, cols] + u_scr[ct, pl.ds(HALO - 1, tm), :] * cw_ref[0:1, cols]
            for j in range(1, CONV_WIDTH):
                out = out + u_scr[ct, pl.ds(HALO - 1 + j, tm), :] * cw_ref[j:j + 1, cols]
            outs.append(out)
        return jnp.concatenate(outs, axis=1)

    for pair in range(FC // FC_DOWN):
        acts = []
        for s in range(FC_DOWN // FC_SUB):
            n = pair * (FC_DOWN // FC_SUB) + s
            start = n * FC_SUB
            act = (_gelu_tanh(conv(wg_ref, cwg_ref, cbg_ref, start, u_ref.at[n, 0]))
                   * conv(wv_ref, cwv_ref, cbv_ref, start, u_ref.at[n, 1]))
            acts.append(act.astype(jnp.bfloat16))
        rows = slice(pair * FC_DOWN, (pair + 1) * FC_DOWN)
        o_ref[...] += jnp.dot(jnp.concatenate(acts, axis=1), wd_ref[rows, :],
                              preferred_element_type=jnp.float32)


def _ffn(x2, norm_g, w_up, conv_w, conv_b, w_down, seq, layer):
    tokens = x2.shape[0]
    tm = TM_FFN
    tiles_per_seq = seq // tm
    nf = D_FF // FC
    halo_blocks = tm // HALO
    last_halo = tokens // HALO - 1
    in_specs = [
        pl.BlockSpec((tm, D_MODEL), lambda i, f: (i, 0)),
        pl.BlockSpec((HALO, D_MODEL), lambda i, f: (jnp.maximum(i * halo_blocks - 1, 0), 0)),
        pl.BlockSpec((HALO, D_MODEL), lambda i, f: (jnp.minimum((i + 1) * halo_blocks, last_halo), 0)),
        pl.BlockSpec((None, 1, D_MODEL), lambda i, f: (layer, 0, 0)),
        pl.BlockSpec((None, D_MODEL, FC), lambda i, f: (layer, 0, f)),
        pl.BlockSpec((None, D_MODEL, FC), lambda i, f: (layer, 0, nf + f)),
        pl.BlockSpec((None, CONV_WIDTH, FC), lambda i, f: (layer, 0, f)),
        pl.BlockSpec((None, CONV_WIDTH, FC), lambda i, f: (layer, 0, nf + f)),
        pl.BlockSpec((None, 1, FC), lambda i, f: (layer, 0, f)),
        pl.BlockSpec((None, 1, FC), lambda i, f: (layer, 0, nf + f)),
        pl.BlockSpec((None, FC, D_MODEL), lambda i, f: (layer, f, 0)),
    ]
    return pl.pallas_call(
        functools.partial(_ffn_kernel, tiles_per_seq=tiles_per_seq),
        grid=(tokens // tm, nf),
        in_specs=in_specs,
        out_specs=pl.BlockSpec((tm, D_MODEL), lambda i, f: (i, 0)),
        out_shape=jax.ShapeDtypeStruct(x2.shape, jnp.float32),
        scratch_shapes=[pltpu.VMEM((tm + 2 * HALO, D_MODEL), jnp.bfloat16),
                        pltpu.VMEM((FC // FC_SUB, 2, FC_SUB // LANES, tm + 2 * HALO, LANES), jnp.float32)],
        compiler_params=_params("parallel", "arbitrary"),
        name="ffn",
    )(x2, x2, x2, norm_g, w_up, w_up, conv_w, conv_w, conv_b, conv_b, w_down)


def kernel(x, w_in, b_gate, qk_gain, rel_pos_bias, w_branch, w_out, norm_mix, norm_ffn, w_up, conv_w,
           conv_b, w_down):
    batch, seq, d_model = x.shape
    depth = w_in.shape[0]
    assert d_model == D_MODEL and seq % TM_FFN == 0 and seq % A_TILE == 0 and seq % (A_CLASSES * 16) == 0
    bf16 = jnp.bfloat16
    x2 = x.reshape(batch * seq, d_model)

    w_qkv = w_in[:, :, :QKV_WIDTH].astype(bf16)
    w_gate = w_in[:, :, QKV_WIDTH:].astype(bf16)
    gains = jnp.tile(qk_gain.reshape(depth, 2 * N_BRANCH, HEAD_DIM), (1, 1, LANES // HEAD_DIM))
    w_br, w_out_bf, w_up_bf, w_down_bf = (w.astype(bf16) for w in (w_branch, w_out, w_up, w_down))
    norm_mix3, norm_ffn3 = norm_mix[:, None, :], norm_ffn[:, None, :]
    b_gate3, conv_b3 = b_gate[:, None, :], conv_b[:, None, :]
    mavg = jnp.asarray(_head_mean_matrix(), bf16)
    ropes = _rope_tables(seq)
    far_bias = jnp.asarray(_dilated_far_bias(seq))
    near_bias = jnp.asarray(_dilated_near_bias())

    for l in range(depth):
        (qa, ka, va, qb, kb2, vbt, qc, kc, vc), (qa_cm, ka_cm, va_cm) = _in_projection(
            x2, norm_mix3, w_qkv, gains, mavg, ropes, seq, l)
        oa = _dilated_attention(qa, ka, va, qa_cm, ka_cm, va_cm, far_bias, near_bias, seq)
        ob = _gqa_attention(qb, kb2, vbt, seq)
        nbr_bias = _nbr_bias_table(rel_pos_bias, seq // GRID_W, l)
        oc = _nbr_attention(qc, kc, vc, nbr_bias, seq)
        x2 = _merge(x2, norm_mix3, oa, ob, oc, w_gate, b_gate3, w_br, w_out_bf, l)
        x2 = _ffn(x2, norm_ffn3, w_up_bf, conv_w, conv_b3, w_down_bf, seq, l)
    return x2.reshape(batch, seq, d_model)
```

```python
import functools

import numpy as np
import jax
import jax.numpy as jnp
from jax import lax
from jax.experimental import pallas as pl
from jax.experimental.pallas import tpu as pltpu

D_MODEL = 1024
GRID_W = 64
HEAD_DIM = 64
HALF = HEAD_DIM // 2
A_HEADS = 4
B_Q_HEADS = 8
B_KV_HEADS = 2
C_HEADS = 4
A_WIDTH = A_HEADS * HEAD_DIM
B_Q_WIDTH = B_Q_HEADS * HEAD_DIM
B_KV_WIDTH = B_KV_HEADS * HEAD_DIM
C_WIDTH = C_HEADS * HEAD_DIM
N_BRANCH = 3
QKV_WIDTH = 3 * A_WIDTH + B_Q_WIDTH + 2 * B_KV_WIDTH + 3 * C_WIDTH
A_PATTERNS = ((128, 1), (512, 4), (2048, 16))
A_CLASSES = A_PATTERNS[-1][1]
C_WIN_ROWS = 8
C_WIN_COLS = 16
ROPE_THETA = 10000.0
D_FF = 3 * D_MODEL
CONV_WIDTH = 3
EPS = 1e-6
NEG_INF = -1e30
LOG2_E = float(np.log2(np.e))
SCORE_SPREAD_LIMIT = 100.0

LANES = 128
SUBLANES = 8
VMEM_LIMIT = 56 * 1024 * 1024

TM_PROJ = 512
TQ = 256
GQA_KEY_SPLITS = 2
GQA_STEP_ROWS = 512
A_TILE = 256
C_TILE_ROWS = 4
C_KEY_ROWS = 12
TM_FFN = 1024
FC = 1024
FC_SUB = 256
FC_DOWN = 512
HALO = SUBLANES


def _params(*sem):
    return pltpu.CompilerParams(dimension_semantics=sem, vmem_limit_bytes=VMEM_LIMIT)


def _const_spec(shape):
    nd = len(shape)
    return pl.BlockSpec(shape, lambda *_: (0,) * nd, pipeline_mode=pl.Buffered(1))


_SMEM_SPEC = pl.BlockSpec(memory_space=pltpu.SMEM)


def _layer_spec(stacked_shape, layer):
    nd = len(stacked_shape) - 1
    return pl.BlockSpec((None,) + tuple(stacked_shape[1:]), lambda *_: (layer,) + (0,) * nd,
                        pipeline_mode=pl.Buffered(1))


def _head_mean_matrix():
    head = np.arange(2 * LANES) // HEAD_DIM
    return (head[:, None] == head[None, :]).astype(np.float32) / HEAD_DIM


def _log2_count(count):
    return np.where(count > 0, np.log2(np.maximum(count, 1).astype(np.float32)),
                    np.float32(NEG_INF)).astype(np.float32)


def _dilated_near_bias():
    a = np.arange(A_TILE)
    tiles = []
    for idx in range(3):
        d = (1 - idx) * A_TILE + a[:, None] - a[None, :]
        count = np.zeros(d.shape, np.int32)
        for window, dil in A_PATTERNS[:-1]:
            assert window // 2 <= A_TILE
            count += ((d % dil == 0) & (np.abs(d) <= window // 2)).astype(np.int32)
        tiles.append(np.tile(_log2_count(count), (2, 1)))
    return np.stack(tiles)


def _dilated_far_bias(seq):
    window, dil = A_PATTERNS[-1]
    class_len = seq // dil
    assert A_TILE % class_len == 0
    a = np.arange(A_TILE)
    same = (a[:, None] // class_len) == (a[None, :] // class_len)
    near = np.abs(a[:, None] % class_len - a[None, :] % class_len) <= window // (2 * dil)
    return np.tile(_log2_count((same & near).astype(np.int32)), (2, 1))


def _rope_tables(seq):
    t = jnp.arange(seq)

    def angles(pos, dim):
        inv = ROPE_THETA ** (-jnp.arange(0, dim, 2, dtype=jnp.float32) / dim)
        return pos[:, None] * inv[None, :]

    ang_1d = angles(t.astype(jnp.float32), HEAD_DIM)
    ang_2d = jnp.concatenate([angles((t // GRID_W).astype(jnp.float32), HEAD_DIM // 2),
                              angles((t % GRID_W).astype(jnp.float32), HEAD_DIM // 2)], axis=-1)
    lane = np.arange(LANES)
    idx = lane % HALF
    sign = np.where(lane % HEAD_DIM < HALF, -1.0, 1.0).astype(np.float32)
    out = []
    for ang in (ang_1d, ang_2d):
        out += [jnp.cos(ang)[:, idx], jnp.sin(ang)[:, idx] * sign]
    return out


def _rms_normed_bf16(x, g):
    ms = jnp.mean(x * x, axis=-1, keepdims=True)
    return (x * lax.rsqrt(ms + EPS) * g).astype(jnp.bfloat16)


def _inproj_kernel(x_ref, g_ref, w_ref, gain_ref, mavg_ref, cos_a_ref, sin_a_ref,
                   cos_b_ref, sin_b_ref,
                   qa_ref, ka_ref, va_ref, qb_ref, kb_ref, vbt_ref, qc_ref, kc_ref, vc_ref,
                   qa_cm_ref, ka_cm_ref, va_cm_ref, cm_ref):
    tm = x_ref.shape[0]
    h = _rms_normed_bf16(x_ref[...], g_ref[...])
    scale = HEAD_DIM ** -0.5 * LOG2_E
    lane = lax.broadcasted_iota(jnp.int32, (tm, LANES), 1)
    head0 = lane < HEAD_DIM
    first_half = lane % HEAD_DIM < HALF
    staged = [0]

    def project(start, width):
        return jnp.dot(h, w_ref[:, start:start + width], preferred_element_type=jnp.float32)

    def normed(y, gain_row, rope, is_q):
        width = y.shape[1]
        ms = jnp.dot((y * y).astype(jnp.bfloat16), mavg_ref[0:width, 0:width],
                     preferred_element_type=jnp.float32)
        y = y * lax.rsqrt(ms + EPS)
        gain = gain_ref[gain_row:gain_row + 1, :]
        chunks = []
        for c in range(width // LANES):
            t = y[:, c * LANES:(c + 1) * LANES] * gain
            if rope is not None:
                partner = jnp.where(first_half, pltpu.roll(t, LANES - HALF, axis=1), pltpu.roll(t, HALF, axis=1))
                t = t * rope[0][...] + partner * rope[1][...]
            chunks.append(t * scale if is_q else t)
        return chunks

    def plain(y):
        return [y[:, c * LANES:(c + 1) * LANES] for c in range(y.shape[1] // LANES)]

    def emit(chunks, out_ref, cm_out_ref=None):
        for c, t in enumerate(chunks):
            lanes = slice(c * LANES, (c + 1) * LANES)
            out_ref[:, lanes] = t.astype(out_ref.dtype)
            if cm_out_ref is not None:
                stage = cm_ref.at[staged[0]]
                staged[0] += 1
                stage[...] = t
                for r in range(A_CLASSES):
                    cm_out_ref[r, :, lanes] = stage[pl.ds(r, tm // A_CLASSES, stride=A_CLASSES), :].astype(
                        cm_out_ref.dtype)

    rope_a, rope_b = (cos_a_ref, sin_a_ref), (cos_b_ref, sin_b_ref)
    y = project(0, 3 * A_WIDTH)
    emit(normed(y[:, 0:A_WIDTH], 0, rope_a, True), qa_ref, qa_cm_ref)
    emit(normed(y[:, A_WIDTH:2 * A_WIDTH], 1, rope_a, False), ka_ref, ka_cm_ref)
    emit(plain(y[:, 2 * A_WIDTH:]), va_ref, va_cm_ref)

    y = project(3 * A_WIDTH, B_Q_WIDTH + 2 * B_KV_WIDTH)
    pair = 2 * LANES
    emit(normed(y[:, 0:pair], 2, rope_b, True) + normed(y[:, pair:B_Q_WIDTH], 2, rope_b, True), qb_ref)
    (kb,) = normed(y[:, B_Q_WIDTH:B_Q_WIDTH + B_KV_WIDTH], 3, rope_b, False)
    swapped = pltpu.roll(kb, HEAD_DIM, axis=1)
    emit([jnp.where(head0, kb, swapped), jnp.where(head0, swapped, kb)], kb_ref)
    vbt_ref[...] = y[:, B_Q_WIDTH + B_KV_WIDTH:].T.astype(vbt_ref.dtype)

    y = project(3 * A_WIDTH + B_Q_WIDTH + 2 * B_KV_WIDTH, 3 * C_WIDTH)
    emit(normed(y[:, 0:C_WIDTH], 4, None, True), qc_ref)
    emit(normed(y[:, C_WIDTH:2 * C_WIDTH], 5, None, False), kc_ref)
    emit(plain(y[:, 2 * C_WIDTH:]), vc_ref)


def _in_projection(x2, norm_g, w_qkv, gains, mavg, ropes, seq, layer):
    tokens = x2.shape[0]
    tm = TM_PROJ
    tiles_per_seq = seq // tm
    row = lambda width: pl.BlockSpec((tm, width), lambda i: (i, 0))
    rope_spec = pl.BlockSpec((tm, LANES), lambda i: (i % tiles_per_seq, 0))
    bf16 = jnp.bfloat16
    cm_shape = (tokens // seq, A_CLASSES, seq // A_CLASSES, A_WIDTH)
    cm_spec = pl.BlockSpec((None, A_CLASSES, tm // A_CLASSES, A_WIDTH),
                           lambda i: (i // tiles_per_seq, 0, i % tiles_per_seq, 0))
    outs = [((tokens, A_WIDTH), row(A_WIDTH))] * 3
    outs += [((tokens, B_Q_WIDTH), row(B_Q_WIDTH)), ((tokens, 2 * B_KV_WIDTH), row(2 * B_KV_WIDTH)),
             ((B_KV_WIDTH, tokens), pl.BlockSpec((B_KV_WIDTH, tm), lambda i: (0, i)))]
    outs += [((tokens, C_WIDTH), row(C_WIDTH))] * 3
    outs += [(cm_shape, cm_spec)] * 3
    res = pl.pallas_call(
        _inproj_kernel,
        grid=(tokens // tm,),
        in_specs=[row(D_MODEL), _layer_spec(norm_g.shape, layer), _layer_spec(w_qkv.shape, layer),
                  _layer_spec(gains.shape, layer), _const_spec(mavg.shape),
                  rope_spec, rope_spec, rope_spec, rope_spec],
        out_specs=[spec for _, spec in outs],
        out_shape=[jax.ShapeDtypeStruct(shape, bf16) for shape, _ in outs],
        scratch_shapes=[pltpu.VMEM((3 * A_WIDTH // LANES, tm, LANES), jnp.float32)],
        compiler_params=_params("parallel"),
        name="in_projection",
    )(x2, norm_g, w_qkv, gains, mavg, *ropes)
    return res[:9], [o.reshape(tokens, A_WIDTH) for o in res[9:]]


def _exact_softmax_units(n_units, score_fn, pv_fn, emit_fn, s_ref, key_axis, key_splits=1):
    def scores(u):
        s = score_fn(u)
        buf = s_ref.at[(u % 2,) + tuple(slice(0, n) for n in s.shape)]
        buf[...] = s
        return buf, s.max(axis=key_axis, keepdims=True)

    ahead = scores(0)
    for u in range(n_units):
        buf, m = ahead
        if u + 1 < n_units:
            ahead = scores(u + 1)
        step = buf.shape[key_axis] // key_splits
        acc = den = None
        for piece in range(key_splits):
            keys = slice(piece * step, (piece + 1) * step)
            p = jnp.exp2((buf[keys, :] if key_axis == 0 else buf[:, keys]) - m)
            part_den = p.sum(axis=key_axis, keepdims=True)
            part = pv_fn(u, keys, p.astype(jnp.bfloat16))
            acc = part if acc is None else acc + part
            den = part_den if den is None else den + part_den
        emit_fn(u, acc, den, m)


def _bounded_softmax_units(n_units, score_fn, pv_fn, emit_fn, bound, key_axis):
    for u in range(n_units):
        p = jnp.exp2(score_fn(u) - bound)
        den = p.sum(axis=key_axis, keepdims=True)
        emit_fn(u, pv_fn(u, slice(None), p.astype(jnp.bfloat16)), den, None)


def _softmax_units(bound, s_ref, n_units, score_fn, pv_fn, emit_fn, key_axis, key_splits=1):
    if s_ref is None:
        _bounded_softmax_units(n_units, score_fn, pv_fn, emit_fn, bound, key_axis)
    else:
        _exact_softmax_units(n_units, score_fn, pv_fn, emit_fn, s_ref, key_axis, key_splits)


def _head0_lanes(shape):
    return lax.broadcasted_iota(jnp.int32, shape, 1) < HEAD_DIM


def _nt_dot(a, b):
    return lax.dot_general(a, b, (((1,), (1,)), ((), ())), preferred_element_type=jnp.float32)


def _stack_heads(q, head0):
    zero = jnp.zeros_like(q)
    return jnp.concatenate([jnp.where(head0, q, zero), jnp.where(head0, zero, q)], axis=0)


def _unstack_heads(x, head0):
    n = x.shape[0] // 2
    x = jnp.broadcast_to(x, (2 * n, LANES))
    return jnp.where(head0, x[:n], x[n:])


def _gqa_attn_kernel(bound_ref, q_ref, k_ref, vt_ref, o_ref, s_ref=None):
    nq, groups = q_ref.shape[1] // LANES, k_ref.shape[1] // LANES
    head0 = _head0_lanes((TQ, LANES))

    def where(u):
        tile, c = divmod(u, nq)
        return slice(tile * TQ, (tile + 1) * TQ), slice(c * LANES, (c + 1) * LANES), c * groups // nq

    def scores(u):
        rows, lanes, g = where(u)
        qs = _stack_heads(q_ref[rows, lanes], head0)
        return _nt_dot(k_ref[:, g * LANES:(g + 1) * LANES], qs)

    def pv(u, keys, p):
        _, _, g = where(u)
        return jnp.dot(vt_ref[g * HEAD_DIM:(g + 1) * HEAD_DIM, keys], p, preferred_element_type=jnp.float32)

    def emit(u, acc, den, m):
        rows, lanes, _ = where(u)
        o_t = acc / den
        both = jnp.concatenate([o_t[:, :TQ], o_t[:, TQ:]], axis=0)
        o_ref[rows, lanes] = both.T.astype(o_ref.dtype)

    _softmax_units(bound_ref[0], s_ref, (q_ref.shape[0] // TQ) * nq, scores, pv, emit, 0, GQA_KEY_SPLITS)


def _gqa_attention(bound, q, k2, vt, seq, bounded):
    tokens = q.shape[0]
    steps = seq // GQA_STEP_ROWS
    q_spec = pl.BlockSpec((GQA_STEP_ROWS, q.shape[1]), lambda b, i: (b * steps + i, 0))
    return pl.pallas_call(
        _gqa_attn_kernel,
        grid=(tokens // seq, steps),
        in_specs=[_SMEM_SPEC, q_spec, pl.BlockSpec((seq, k2.shape[1]), lambda b, i: (b, 0)),
                  pl.BlockSpec((vt.shape[0], seq), lambda b, i: (0, b))],
        out_specs=q_spec,
        out_shape=jax.ShapeDtypeStruct(q.shape, jnp.bfloat16),
        scratch_shapes=[] if bounded else [pltpu.VMEM((2, seq, 2 * TQ), jnp.float32)],
        compiler_params=_params("parallel", "parallel"),
        name="gqa_attention" + ("" if bounded else "_exact"),
    )(bound, q, k2, vt)


def _dilated_attn_kernel(bound_ref, q_ref, k_ref, v_ref, q_cm_ref, k_cm_ref, v_cm_ref, far_ref, near_ref, o_ref,
                         acc_ref, den_ref, max_ref=None, s_ref=None):
    seq = q_ref.shape[0]
    n_chunks = q_ref.shape[1] // LANES
    tile = A_TILE
    n_tiles = seq // tile
    class_len = seq // A_CLASSES
    head0 = _head0_lanes((tile, LANES))
    bound = bound_ref[0]

    def far_where(u):
        t, c = divmod(u, n_chunks)
        return t, c, slice(t * tile, (t + 1) * tile), slice(c * LANES, (c + 1) * LANES)

    def far_scores(u):
        _, _, rows, lanes = far_where(u)
        return _nt_dot(_stack_heads(q_cm_ref[rows, lanes], head0), k_cm_ref[rows, lanes]) + far_ref[...]

    def far_pv(u, keys, p):
        _, _, rows, lanes = far_where(u)
        return jnp.dot(p, v_cm_ref[rows, lanes][keys], preferred_element_type=jnp.float32)

    def far_emit(u, pv, den, m):
        t, c, _, _ = far_where(u)
        for ref, stat in ((acc_ref, pv), (den_ref, den), (max_ref, m)):
            if stat is None:
                continue
            stat = _unstack_heads(stat, head0)
            for e in range(tile // class_len):
                cls = t * (tile // class_len) + e
                ref[c, pl.ds(cls, class_len, stride=A_CLASSES), :] = stat[e * class_len:(e + 1) * class_len]

    _softmax_units(bound, s_ref, n_tiles * n_chunks, far_scores, far_pv, far_emit, 1)

    def near_where(u):
        i, c = divmod(u, n_chunks)
        return i, c, max(i - 1, 0), min(i + 1, n_tiles - 1), slice(c * LANES, (c + 1) * LANES)

    def near_scores(u):
        i, _, lo, hi, lanes = near_where(u)
        qs = _stack_heads(q_ref[i * tile:(i + 1) * tile, lanes], head0)
        return jnp.concatenate([_nt_dot(qs, k_ref[j * tile:(j + 1) * tile, lanes]) + near_ref[j - i + 1]
                                for j in range(lo, hi + 1)], axis=1)

    def near_pv(u, keys, p):
        _, _, lo, hi, lanes = near_where(u)
        return jnp.dot(p, v_ref[lo * tile:(hi + 1) * tile, lanes][keys], preferred_element_type=jnp.float32)

    def near_emit(u, pv, den, m):
        i, c, _, _, lanes = near_where(u)
        rows = slice(i * tile, (i + 1) * tile)
        acc, den = _unstack_heads(pv, head0), _unstack_heads(den, head0)
        far_acc, far_den = acc_ref[c, rows, :], den_ref[c, rows, :]
        if m is not None:
            m, far_m = _unstack_heads(m, head0), max_ref[c, rows, :]
            top = jnp.maximum(m, far_m)
            w_near, w_far = jnp.exp2(m - top), jnp.exp2(far_m - top)
            acc, den, far_acc, far_den = acc * w_near, den * w_near, far_acc * w_far, far_den * w_far
        o_ref[rows, lanes] = ((acc + far_acc) / (den + far_den)).astype(o_ref.dtype)

    _softmax_units(bound, s_ref, n_tiles * n_chunks, near_scores, near_pv, near_emit, 1)


def _dilated_attention(bound, q, k, v, q_cm, k_cm, v_cm, far_bias, near_bias, seq, bounded):
    tokens, width = q.shape
    seq_spec = pl.BlockSpec((seq, width), lambda b: (b, 0))
    stat = pltpu.VMEM((width // LANES, seq, LANES), jnp.float32)
    exact_scratch = [stat, pltpu.VMEM((2, 2 * A_TILE, 3 * A_TILE), jnp.float32)]
    return pl.pallas_call(
        _dilated_attn_kernel,
        grid=(tokens // seq,),
        in_specs=[_SMEM_SPEC] + [seq_spec] * 6 + [_const_spec(far_bias.shape), _const_spec(near_bias.shape)],
        out_specs=seq_spec,
        out_shape=jax.ShapeDtypeStruct(q.shape, jnp.bfloat16),
        scratch_shapes=[stat, stat] + ([] if bounded else exact_scratch),
        compiler_params=_params("parallel"),
        name="dilated_attention" + ("" if bounded else "_exact"),
    )(bound, q, k, v, q_cm, k_cm, v_cm, far_bias, near_bias)


def _c_geometry(rows_total):
    n_tiles = rows_total // C_TILE_ROWS
    wr = min(C_WIN_ROWS, rows_total)
    key_units = C_KEY_ROWS // C_TILE_ROWS
    return n_tiles, wr, key_units


def _nbr_bias_kernel(rpb_ref, o_ref, *, rows_total, layer):
    n_tiles, wr, key_units = _c_geometry(rows_total)
    h = pl.program_id(0) + layer * pl.num_programs(0)
    shape = (GRID_W, LANES)
    c = lax.broadcasted_iota(jnp.int32, shape, 0)
    lane = lax.broadcasted_iota(jnp.int32, shape, 1)
    kc = lane % GRID_W
    d = kc - c
    start = jnp.clip(c - C_WIN_COLS // 2, 0, GRID_W - C_WIN_COLS)
    in_window = (kc >= start) & (kc < start + C_WIN_COLS)
    neg = jnp.full(shape, NEG_INF, jnp.float32)
    n_dr = 2 * C_WIN_ROWS - 1
    n_dc = 2 * C_WIN_COLS - 1
    col_bias = []
    for dri in range(n_dr):
        t = neg
        for j in range(n_dc):
            t = jnp.where(d == j - (C_WIN_COLS - 1), rpb_ref[h * n_dr + dri, j], t)
        col_bias.append(jnp.where(in_window, t * LOG2_E, neg))
    left = lane < GRID_W
    for case, tile in enumerate((0, 1, n_tiles - 1)):
        key_row0 = C_TILE_ROWS * min(max(tile - 1, 0), n_tiles - key_units)
        for rq in range(C_TILE_ROWS):
            r = tile * C_TILE_ROWS + rq
            first = min(max(r - wr // 2, 0), rows_total - wr)
            for pair in range(C_KEY_ROWS // 2):
                halves = []
                for e in range(2):
                    kr = key_row0 + 2 * pair + e
                    halves.append(col_bias[kr - r + C_WIN_ROWS - 1] if first <= kr < first + wr else neg)
                o_ref[case, 0, rq * GRID_W:(rq + 1) * GRID_W, pair * LANES:(pair + 1) * LANES] = (
                    jnp.where(left, halves[0], halves[1]))


def _nbr_bias_table(rpb, rows_total, layer):
    depth, heads, n_dr, n_dc = rpb.shape
    tq = C_TILE_ROWS * GRID_W
    tk = C_KEY_ROWS * GRID_W
    return pl.pallas_call(
        functools.partial(_nbr_bias_kernel, rows_total=rows_total, layer=layer),
        grid=(heads,),
        in_specs=[pl.BlockSpec(memory_space=pltpu.SMEM)],
        out_specs=pl.BlockSpec((3, 1, tq, tk), lambda h: (0, h, 0, 0)),
        out_shape=jax.ShapeDtypeStruct((3, heads, tq, tk), jnp.float32),
        compiler_params=_params("parallel"),
        name="nbr_bias_table",
    )(rpb.reshape(depth * heads * n_dr, n_dc))


def _nbr_attn_kernel(bound_ref, q_ref, k_ref, v_ref, bias_ref, o_ref, s_ref=None, *, rows_total):
    n_tiles, _, key_units = _c_geometry(rows_total)
    tq = C_TILE_ROWS * GRID_W
    n_heads = 2 * (q_ref.shape[1] // LANES)
    head0 = _head0_lanes((tq, LANES))

    def where(u):
        tile, head = divmod(u, n_heads)
        key0 = tq * min(max(tile - 1, 0), n_tiles - key_units)
        case = 0 if tile == 0 else (2 if tile == n_tiles - 1 else 1)
        lanes = slice((head // 2) * LANES, (head // 2 + 1) * LANES)
        return tile, head, slice(key0, key0 + key_units * tq), case, lanes

    def scores(u):
        tile, head, keys, case, lanes = where(u)
        q = q_ref[tile * tq:(tile + 1) * tq, lanes]
        qm = jnp.where(head0 if head % 2 == 0 else ~head0, q, jnp.zeros_like(q))
        return _nt_dot(qm, k_ref[keys, lanes]) + bias_ref[case, head]

    def pv(u, piece, p):
        _, _, keys, _, lanes = where(u)
        return jnp.dot(p, v_ref[keys, lanes][piece], preferred_element_type=jnp.float32)

    pair = []

    def emit(u, acc, den, m):
        tile, head, _, _, lanes = where(u)
        pair.append(acc / den)
        if head % 2 == 1:
            o_ref[tile * tq:(tile + 1) * tq, lanes] = jnp.where(head0, *pair).astype(o_ref.dtype)
            pair.clear()

    _softmax_units(bound_ref[0], s_ref, n_tiles * n_heads, scores, pv, emit, 1)


def _nbr_attention(bound, q, k, v, bias, seq, bounded):
    tokens, width = q.shape
    rows_total = seq // GRID_W
    _, _, key_units = _c_geometry(rows_total)
    tq = C_TILE_ROWS * GRID_W
    seq_spec = pl.BlockSpec((seq, width), lambda b: (b, 0))
    return pl.pallas_call(
        functools.partial(_nbr_attn_kernel, rows_total=rows_total),
        grid=(tokens // seq,),
        in_specs=[_SMEM_SPEC, seq_spec, seq_spec, seq_spec, _const_spec(bias.shape)],
        out_specs=seq_spec,
        out_shape=jax.ShapeDtypeStruct(q.shape, jnp.bfloat16),
        scratch_shapes=[] if bounded else [pltpu.VMEM((2, tq, key_units * tq), jnp.float32)],
        compiler_params=_params("parallel"),
        name="nbr_attention" + ("" if bounded else "_exact"),
    )(bound, q, k, v, bias)


def _merge_kernel(x_ref, g_ref, oa_ref, ob_ref, oc_ref, wg_ref, bg_ref, wbr_ref, wo_ref, o_ref):
    x = x_ref[...]
    h = _rms_normed_bf16(x, g_ref[...])
    merged = None
    row = 0
    for n, br_ref in enumerate((oa_ref, ob_ref, oc_ref)):
        cols = slice(n * D_MODEL, (n + 1) * D_MODEL)
        logits = jnp.dot(h, wg_ref[:, cols], preferred_element_type=jnp.float32) + bg_ref[:, cols]
        gate = 1.0 / (1.0 + jnp.exp(-logits))
        width = br_ref.shape[1]
        y = jnp.dot(br_ref[...], wbr_ref[row:row + width, :], preferred_element_type=jnp.float32)
        row += width
        merged = gate * y if merged is None else merged + gate * y
    o_ref[...] = x + jnp.dot(merged.astype(jnp.bfloat16), wo_ref[...], preferred_element_type=jnp.float32)


def _merge(x2, norm_g, oa, ob, oc, w_gate, b_gate, w_branch, w_out, layer):
    tokens = x2.shape[0]
    tm = TM_PROJ
    row = lambda width: pl.BlockSpec((tm, width), lambda i: (i, 0))
    return pl.pallas_call(
        _merge_kernel,
        grid=(tokens // tm,),
        in_specs=[row(D_MODEL), _layer_spec(norm_g.shape, layer), row(oa.shape[1]), row(ob.shape[1]),
                  row(oc.shape[1]), _layer_spec(w_gate.shape, layer), _layer_spec(b_gate.shape, layer),
                  _layer_spec(w_branch.shape, layer), _layer_spec(w_out.shape, layer)],
        out_specs=row(D_MODEL),
        out_shape=jax.ShapeDtypeStruct(x2.shape, jnp.float32),
        compiler_params=_params("parallel"),
        name="gated_merge",
    )(x2, norm_g, oa, ob, oc, w_gate, b_gate, w_branch, w_out)


def _gelu_tanh(x):
    return 0.5 * x * (1.0 + jnp.tanh(np.sqrt(2.0 / np.pi).astype(np.float32) * (x + 0.044715 * (x * x * x))))


def _ffn_kernel(x_ref, prev_ref, next_ref, g_ref, wg_ref, wv_ref, cwg_ref, cwv_ref, cbg_ref, cbv_ref,
                wd_ref, o_ref, h_ref, u_ref, *, tiles_per_seq):
    i = pl.program_id(0)
    f = pl.program_id(1)
    tm = x_ref.shape[0]

    @pl.when(f == 0)
    def _():
        g = g_ref[...]
        has_prev = (i % tiles_per_seq != 0).astype(jnp.float32)
        has_next = (i % tiles_per_seq != tiles_per_seq - 1).astype(jnp.float32)
        h_ref[0:HALO, :] = _rms_normed_bf16(prev_ref[...] * has_prev, g)
        h_ref[HALO:HALO + tm, :] = _rms_normed_bf16(x_ref[...], g)
        h_ref[HALO + tm:, :] = _rms_normed_bf16(next_ref[...] * has_next, g)
        o_ref[...] = x_ref[...]

    h = h_ref[...]

    def conv(w_ref, cw_ref, cb_ref, start, u_scr):
        u = jnp.dot(h, w_ref[:, start:start + FC_SUB], preferred_element_type=jnp.float32)
        outs = []
        for ct in range(FC_SUB // LANES):
            u_scr[ct] = u[:, ct * LANES:(ct + 1) * LANES]
            cols = slice(start + ct * LANES, start + (ct + 1) * LANES)
            out = cb_ref[:, cols] + u_scr[ct, pl.ds(HALO - 1, tm), :] * cw_ref[0:1, cols]
            for j in range(1, CONV_WIDTH):
                out = out + u_scr[ct, pl.ds(HALO - 1 + j, tm), :] * cw_ref[j:j + 1, cols]
            outs.append(out)
        return jnp.concatenate(outs, axis=1)

    for pair in range(FC // FC_DOWN):
        acts = []
        for s in range(FC_DOWN // FC_SUB):
            n = pair * (FC_DOWN // FC_SUB) + s
            start = n * FC_SUB
            act = (_gelu_tanh(conv(wg_ref, cwg_ref, cbg_ref, start, u_ref.at[n, 0]))
                   * conv(wv_ref, cwv_ref, cbv_ref, start, u_ref.at[n, 1]))
            acts.append(act.astype(jnp.bfloat16))
        rows = slice(pair * FC_DOWN, (pair + 1) * FC_DOWN)
        o_ref[...] += jnp.dot(jnp.concatenate(acts, axis=1), wd_ref[rows, :],
                              preferred_element_type=jnp.float32)


def _ffn(x2, norm_g, w_up, conv_w, conv_b, w_down, seq, layer):
    tokens = x2.shape[0]
    tm = TM_FFN
    tiles_per_seq = seq // tm
    nf = D_FF // FC
    halo_blocks = tm // HALO
    last_halo = tokens // HALO - 1
    in_specs = [
        pl.BlockSpec((tm, D_MODEL), lambda i, f: (i, 0)),
        pl.BlockSpec((HALO, D_MODEL), lambda i, f: (jnp.maximum(i * halo_blocks - 1, 0), 0)),
        pl.BlockSpec((HALO, D_MODEL), lambda i, f: (jnp.minimum((i + 1) * halo_blocks, last_halo), 0)),
        pl.BlockSpec((None, 1, D_MODEL), lambda i, f: (layer, 0, 0)),
        pl.BlockSpec((None, D_MODEL, FC), lambda i, f: (layer, 0, f)),
        pl.BlockSpec((None, D_MODEL, FC), lambda i, f: (layer, 0, nf + f)),
        pl.BlockSpec((None, CONV_WIDTH, FC), lambda i, f: (layer, 0, f)),
        pl.BlockSpec((None, CONV_WIDTH, FC), lambda i, f: (layer, 0, nf + f)),
        pl.BlockSpec((None, 1, FC), lambda i, f: (layer, 0, f)),
        pl.BlockSpec((None, 1, FC), lambda i, f: (layer, 0, nf + f)),
        pl.BlockSpec((None, FC, D_MODEL), lambda i, f: (layer, f, 0)),
    ]
    return pl.pallas_call(
        functools.partial(_ffn_kernel, tiles_per_seq=tiles_per_seq),
        grid=(tokens // tm, nf),
        in_specs=in_specs,
        out_specs=pl.BlockSpec((tm, D_MODEL), lambda i, f: (i, 0)),
        out_shape=jax.ShapeDtypeStruct(x2.shape, jnp.float32),
        scratch_shapes=[pltpu.VMEM((tm + 2 * HALO, D_MODEL), jnp.bfloat16),
                        pltpu.VMEM((FC // FC_SUB, 2, FC_SUB // LANES, tm + 2 * HALO, LANES), jnp.float32)],
        compiler_params=_params("parallel", "arbitrary"),
        name="ffn",
    )(x2, x2, x2, norm_g, w_up, w_up, conv_w, conv_w, conv_b, conv_b, w_down)


def _score_bounds(qk_gain, rel_pos_bias):
    g = jnp.max(jnp.abs(qk_gain), axis=-1)
    qk = 1.05 * HEAD_DIM * (HEAD_DIM ** -0.5 * LOG2_E) * g[..., 0] * g[..., 1]
    rpb = rel_pos_bias.reshape(rel_pos_bias.shape[0], -1) * LOG2_E
    zero = jnp.zeros_like(qk[:, 0])
    bias_hi = jnp.stack([zero + float(np.max(_dilated_near_bias())), zero, rpb.max(axis=1)], axis=1)
    bias_lo = jnp.stack([zero, zero, rpb.min(axis=1)], axis=1)
    return qk + bias_hi, 2.0 * qk + bias_hi - bias_lo <= SCORE_SPREAD_LIMIT


def _either_softmax(within_limit, attention, bound, *args, seq):
    return lax.cond(within_limit, functools.partial(attention, seq=seq, bounded=True),
                    functools.partial(attention, seq=seq, bounded=False), bound.reshape(1), *args)


def kernel(x, w_in, b_gate, qk_gain, rel_pos_bias, w_branch, w_out, norm_mix, norm_ffn, w_up, conv_w,
           conv_b, w_down):
    batch, seq, d_model = x.shape
    depth = w_in.shape[0]
    assert d_model == D_MODEL and seq % TM_FFN == 0 and seq % A_TILE == 0 and seq % (A_CLASSES * 16) == 0
    bf16 = jnp.bfloat16
    x2 = x.reshape(batch * seq, d_model)

    w_qkv = w_in[:, :, :QKV_WIDTH].astype(bf16)
    w_gate = w_in[:, :, QKV_WIDTH:].astype(bf16)
    gains = jnp.tile(qk_gain.reshape(depth, 2 * N_BRANCH, HEAD_DIM), (1, 1, LANES // HEAD_DIM))
    w_br, w_out_bf, w_up_bf, w_down_bf = (w.astype(bf16) for w in (w_branch, w_out, w_up, w_down))
    norm_mix3, norm_ffn3 = norm_mix[:, None, :], norm_ffn[:, None, :]
    b_gate3, conv_b3 = b_gate[:, None, :], conv_b[:, None, :]
    mavg = jnp.asarray(_head_mean_matrix(), bf16)
    ropes = _rope_tables(seq)
    far_bias = jnp.asarray(_dilated_far_bias(seq))
    near_bias = jnp.asarray(_dilated_near_bias())
    bounds, within_limit = _score_bounds(qk_gain, rel_pos_bias)

    for l in range(depth):
        (qa, ka, va, qb, kb2, vbt, qc, kc, vc), (qa_cm, ka_cm, va_cm) = _in_projection(
            x2, norm_mix3, w_qkv, gains, mavg, ropes, seq, l)
        oa = _either_softmax(within_limit[l, 0], _dilated_attention, bounds[l, 0],
                             qa, ka, va, qa_cm, ka_cm, va_cm, far_bias, near_bias, seq=seq)
        ob = _either_softmax(within_limit[l, 1], _gqa_attention, bounds[l, 1], qb, kb2, vbt, seq=seq)
        nbr_bias = _nbr_bias_table(rel_pos_bias, seq // GRID_W, l)
        oc = _either_softmax(within_limit[l, 2], _nbr_attention, bounds[l, 2], qc, kc, vc, nbr_bias, seq=seq)
        x2 = _merge(x2, norm_mix3, oa, ob, oc, w_gate, b_gate3, w_br, w_out_bf, l)
        x2 = _ffn(x2, norm_ffn3, w_up_bf, conv_w, conv_b3, w_down_bf, seq, l)
    return x2.reshape(batch, seq, d_model)
```

```python
import functools

import numpy as np
import jax
import jax.numpy as jnp
from jax import lax
from jax.experimental import pallas as pl
from jax.experimental.pallas import tpu as pltpu

D_MODEL = 1024
GRID_W = 64
HEAD_DIM = 64
HALF = HEAD_DIM // 2
A_HEADS = 4
B_Q_HEADS = 8
B_KV_HEADS = 2
C_HEADS = 4
A_WIDTH = A_HEADS * HEAD_DIM
B_Q_WIDTH = B_Q_HEADS * HEAD_DIM
B_KV_WIDTH = B_KV_HEADS * HEAD_DIM
C_WIDTH = C_HEADS * HEAD_DIM
N_BRANCH = 3
QKV_WIDTH = 3 * A_WIDTH + B_Q_WIDTH + 2 * B_KV_WIDTH + 3 * C_WIDTH
A_PATTERNS = ((128, 1), (512, 4), (2048, 16))
A_CLASSES = A_PATTERNS[-1][1]
C_WIN_ROWS = 8
C_WIN_COLS = 16
ROPE_THETA = 10000.0
D_FF = 3 * D_MODEL
CONV_WIDTH = 3
EPS = 1e-6
NEG_INF = -1e30
LOG2_E = float(np.log2(np.e))
SCORE_SPREAD_LIMIT = 100.0

LANES = 128
SUBLANES = 8
VMEM_LIMIT = 56 * 1024 * 1024

TM_INPROJ = 1024
TM_MERGE = 512
TQ = 256
GQA_KEY_SPLITS = 2
GQA_STEP_ROWS = 1024
A_TILE = 256
C_TILE_ROWS = 4
C_KEY_ROWS = 12
TM_FFN = 1024
FC_SUB = 256
FC_DOWN = 512
FFN_U_SLOTS = 4
HALO = SUBLANES


def _params(*sem):
    return pltpu.CompilerParams(dimension_semantics=sem, vmem_limit_bytes=VMEM_LIMIT)


def _const_spec(shape):
    nd = len(shape)
    return pl.BlockSpec(shape, lambda *_: (0,) * nd, pipeline_mode=pl.Buffered(1))


_SMEM_SPEC = pl.BlockSpec(memory_space=pltpu.SMEM)


def _layer_spec(stacked_shape, layer):
    nd = len(stacked_shape) - 1
    return pl.BlockSpec((None,) + tuple(stacked_shape[1:]), lambda *_: (layer,) + (0,) * nd,
                        pipeline_mode=pl.Buffered(1))


def _head_mean_matrix():
    head = np.arange(2 * LANES) // HEAD_DIM
    return (head[:, None] == head[None, :]).astype(np.float32) / HEAD_DIM


def _log2_count(count):
    return np.where(count > 0, np.log2(np.maximum(count, 1).astype(np.float32)),
                    np.float32(NEG_INF)).astype(np.float32)


def _dilated_near_bias():
    a = np.arange(A_TILE)
    tiles = []
    for idx in range(3):
        d = (1 - idx) * A_TILE + a[:, None] - a[None, :]
        count = np.zeros(d.shape, np.int32)
        for window, dil in A_PATTERNS[:-1]:
            assert window // 2 <= A_TILE
            count += ((d % dil == 0) & (np.abs(d) <= window // 2)).astype(np.int32)
        tiles.append(np.tile(_log2_count(count), (2, 1)))
    return np.stack(tiles)


def _dilated_far_bias(seq):
    window, dil = A_PATTERNS[-1]
    class_len = seq // dil
    assert A_TILE % class_len == 0
    a = np.arange(A_TILE)
    same = (a[:, None] // class_len) == (a[None, :] // class_len)
    near = np.abs(a[:, None] % class_len - a[None, :] % class_len) <= window // (2 * dil)
    return np.tile(_log2_count((same & near).astype(np.int32)), (2, 1))


def _rope_tables(seq):
    t = jnp.arange(seq)

    def angles(pos, dim):
        inv = ROPE_THETA ** (-jnp.arange(0, dim, 2, dtype=jnp.float32) / dim)
        return pos[:, None] * inv[None, :]

    ang_1d = angles(t.astype(jnp.float32), HEAD_DIM)
    ang_2d = jnp.concatenate([angles((t // GRID_W).astype(jnp.float32), HEAD_DIM // 2),
                              angles((t % GRID_W).astype(jnp.float32), HEAD_DIM // 2)], axis=-1)
    lane = np.arange(LANES)
    idx = lane % HALF
    sign = np.where(lane % HEAD_DIM < HALF, -1.0, 1.0).astype(np.float32)
    out = []
    for ang in (ang_1d, ang_2d):
        out += [jnp.cos(ang)[:, idx], jnp.sin(ang)[:, idx] * sign]
    return out


def _rms_normed_bf16(x, g):
    ms = jnp.mean(x * x, axis=-1, keepdims=True)
    return (x * lax.rsqrt(ms + EPS) * g).astype(jnp.bfloat16)


def _inproj_kernel(x_ref, g_ref, w_ref, gain_ref, mavg_ref, cos_a_ref, sin_a_ref,
                   cos_b_ref, sin_b_ref,
                   qa_ref, ka_ref, va_ref, qb_ref, kb_ref, vbt_ref, qc_ref, kc_ref, vc_ref,
                   qa_cm_ref, ka_cm_ref, va_cm_ref, cm_ref):
    tm = x_ref.shape[0]
    h = _rms_normed_bf16(x_ref[...], g_ref[...])
    scale = HEAD_DIM ** -0.5 * LOG2_E
    lane = lax.broadcasted_iota(jnp.int32, (tm, LANES), 1)
    head0 = lane < HEAD_DIM
    first_half = lane % HEAD_DIM < HALF
    staged = [0]

    def project(start, width):
        return jnp.dot(h, w_ref[:, start:start + width], preferred_element_type=jnp.float32)

    def normed(y, gain_row, rope, is_q):
        width = y.shape[1]
        ms = jnp.dot((y * y).astype(jnp.bfloat16), mavg_ref[0:width, 0:width],
                     preferred_element_type=jnp.float32)
        y = y * lax.rsqrt(ms + EPS)
        gain = gain_ref[gain_row:gain_row + 1, :]
        chunks = []
        for c in range(width // LANES):
            t = y[:, c * LANES:(c + 1) * LANES] * gain
            if rope is not None:
                partner = jnp.where(first_half, pltpu.roll(t, LANES - HALF, axis=1), pltpu.roll(t, HALF, axis=1))
                t = t * rope[0][...] + partner * rope[1][...]
            chunks.append(t * scale if is_q else t)
        return chunks

    def plain(y):
        return [y[:, c * LANES:(c + 1) * LANES] for c in range(y.shape[1] // LANES)]

    def emit(chunks, out_ref, cm_out_ref=None):
        for c, t in enumerate(chunks):
            lanes = slice(c * LANES, (c + 1) * LANES)
            out_ref[:, lanes] = t.astype(out_ref.dtype)
            if cm_out_ref is not None:
                stage = cm_ref.at[staged[0]]
                staged[0] += 1
                stage[...] = t
                for r in range(A_CLASSES):
                    cm_out_ref[r, :, lanes] = stage[pl.ds(r, tm // A_CLASSES, stride=A_CLASSES), :].astype(
                        cm_out_ref.dtype)

    rope_a, rope_b = (cos_a_ref, sin_a_ref), (cos_b_ref, sin_b_ref)
    y = project(0, 3 * A_WIDTH)
    emit(normed(y[:, 0:A_WIDTH], 0, rope_a, True), qa_ref, qa_cm_ref)
    emit(normed(y[:, A_WIDTH:2 * A_WIDTH], 1, rope_a, False), ka_ref, ka_cm_ref)
    emit(plain(y[:, 2 * A_WIDTH:]), va_ref, va_cm_ref)

    y = project(3 * A_WIDTH, B_Q_WIDTH + 2 * B_KV_WIDTH)
    pair = 2 * LANES
    emit(normed(y[:, 0:pair], 2, rope_b, True) + normed(y[:, pair:B_Q_WIDTH], 2, rope_b, True), qb_ref)
    (kb,) = normed(y[:, B_Q_WIDTH:B_Q_WIDTH + B_KV_WIDTH], 3, rope_b, False)
    swapped = pltpu.roll(kb, HEAD_DIM, axis=1)
    emit([jnp.where(head0, kb, swapped), jnp.where(head0, swapped, kb)], kb_ref)
    vbt_ref[...] = y[:, B_Q_WIDTH + B_KV_WIDTH:].T.astype(vbt_ref.dtype)

    y = project(3 * A_WIDTH + B_Q_WIDTH + 2 * B_KV_WIDTH, 3 * C_WIDTH)
    emit(normed(y[:, 0:C_WIDTH], 4, None, True), qc_ref)
    emit(normed(y[:, C_WIDTH:2 * C_WIDTH], 5, None, False), kc_ref)
    emit(plain(y[:, 2 * C_WIDTH:]), vc_ref)


def _in_projection(x2, norm_g, w_qkv, gains, mavg, ropes, seq, layer):
    tokens = x2.shape[0]
    tm = TM_INPROJ
    tiles_per_seq = seq // tm
    row = lambda width: pl.BlockSpec((tm, width), lambda i: (i, 0))
    rope_spec = pl.BlockSpec((tm, LANES), lambda i: (i % tiles_per_seq, 0))
    bf16 = jnp.bfloat16
    cm_shape = (tokens // seq, A_CLASSES, seq // A_CLASSES, A_WIDTH)
    cm_spec = pl.BlockSpec((None, A_CLASSES, tm // A_CLASSES, A_WIDTH),
                           lambda i: (i // tiles_per_seq, 0, i % tiles_per_seq, 0))
    outs = [((tokens, A_WIDTH), row(A_WIDTH))] * 3
    outs += [((tokens, B_Q_WIDTH), row(B_Q_WIDTH)), ((tokens, 2 * B_KV_WIDTH), row(2 * B_KV_WIDTH)),
             ((B_KV_WIDTH, tokens), pl.BlockSpec((B_KV_WIDTH, tm), lambda i: (0, i)))]
    outs += [((tokens, C_WIDTH), row(C_WIDTH))] * 3
    outs += [(cm_shape, cm_spec)] * 3
    res = pl.pallas_call(
        _inproj_kernel,
        grid=(tokens // tm,),
        in_specs=[row(D_MODEL), _layer_spec(norm_g.shape, layer), _layer_spec(w_qkv.shape, layer),
                  _layer_spec(gains.shape, layer), _const_spec(mavg.shape),
                  rope_spec, rope_spec, rope_spec, rope_spec],
        out_specs=[spec for _, spec in outs],
        out_shape=[jax.ShapeDtypeStruct(shape, bf16) for shape, _ in outs],
        scratch_shapes=[pltpu.VMEM((3 * A_WIDTH // LANES, tm, LANES), jnp.float32)],
        compiler_params=_params("parallel"),
        name="in_projection",
    )(x2, norm_g, w_qkv, gains, mavg, *ropes)
    return res[:9], [o.reshape(tokens, A_WIDTH) for o in res[9:]]


def _exact_softmax_units(n_units, score_fn, pv_fn, emit_fn, s_ref, key_axis, key_splits=1):
    def scores(u):
        s = score_fn(u)
        buf = s_ref.at[(u % 2,) + tuple(slice(0, n) for n in s.shape)]
        buf[...] = s
        return buf, s.max(axis=key_axis, keepdims=True)

    ahead = scores(0)
    for u in range(n_units):
        buf, m = ahead
        if u + 1 < n_units:
            ahead = scores(u + 1)
        step = buf.shape[key_axis] // key_splits
        acc = den = None
        for piece in range(key_splits):
            keys = slice(piece * step, (piece + 1) * step)
            p = jnp.exp2((buf[keys, :] if key_axis == 0 else buf[:, keys]) - m)
            part_den = p.sum(axis=key_axis, keepdims=True)
            part = pv_fn(u, keys, p.astype(jnp.bfloat16))
            acc = part if acc is None else acc + part
            den = part_den if den is None else den + part_den
        emit_fn(u, acc, den, m)


def _bounded_softmax_units(n_units, score_fn, pv_fn, emit_fn, bound, key_axis):
    for u in range(n_units):
        p = jnp.exp2(score_fn(u) - bound)
        den = p.sum(axis=key_axis, keepdims=True)
        emit_fn(u, pv_fn(u, slice(None), p.astype(jnp.bfloat16)), den, None)


def _softmax_units(bound, s_ref, n_units, score_fn, pv_fn, emit_fn, key_axis, key_splits=1):
    if s_ref is None:
        _bounded_softmax_units(n_units, score_fn, pv_fn, emit_fn, bound, key_axis)
    else:
        _exact_softmax_units(n_units, score_fn, pv_fn, emit_fn, s_ref, key_axis, key_splits)


def _head0_lanes(shape):
    return lax.broadcasted_iota(jnp.int32, shape, 1) < HEAD_DIM


def _nt_dot(a, b):
    return lax.dot_general(a, b, (((1,), (1,)), ((), ())), preferred_element_type=jnp.float32)


def _stack_heads(q, head0):
    zero = jnp.zeros_like(q)
    return jnp.concatenate([jnp.where(head0, q, zero), jnp.where(head0, zero, q)], axis=0)


def _unstack_heads(x, head0):
    n = x.shape[0] // 2
    x = jnp.broadcast_to(x, (2 * n, LANES))
    return jnp.where(head0, x[:n], x[n:])


def _gqa_attn_kernel(bound_ref, q_ref, k_ref, vt_ref, o_ref, s_ref=None):
    nq, groups = q_ref.shape[1] // LANES, k_ref.shape[1] // LANES
    head0 = _head0_lanes((TQ, LANES))

    def where(u):
        tile, c = divmod(u, nq)
        return slice(tile * TQ, (tile + 1) * TQ), slice(c * LANES, (c + 1) * LANES), c * groups // nq

    def scores(u):
        rows, lanes, g = where(u)
        qs = _stack_heads(q_ref[rows, lanes], head0)
        return _nt_dot(k_ref[:, g * LANES:(g + 1) * LANES], qs)

    def pv(u, keys, p):
        _, _, g = where(u)
        return jnp.dot(vt_ref[g * HEAD_DIM:(g + 1) * HEAD_DIM, keys], p, preferred_element_type=jnp.float32)

    def emit(u, acc, den, m):
        rows, lanes, _ = where(u)
        o_t = acc / den
        both = jnp.concatenate([o_t[:, :TQ], o_t[:, TQ:]], axis=0)
        o_ref[rows, lanes] = both.T.astype(o_ref.dtype)

    _softmax_units(bound_ref[0], s_ref, (q_ref.shape[0] // TQ) * nq, scores, pv, emit, 0, GQA_KEY_SPLITS)


def _gqa_attention(bound, q, k2, vt, seq, bounded):
    tokens = q.shape[0]
    steps = seq // GQA_STEP_ROWS
    q_spec = pl.BlockSpec((GQA_STEP_ROWS, q.shape[1]), lambda b, i: (b * steps + i, 0))
    return pl.pallas_call(
        _gqa_attn_kernel,
        grid=(tokens // seq, steps),
        in_specs=[_SMEM_SPEC, q_spec, pl.BlockSpec((seq, k2.shape[1]), lambda b, i: (b, 0)),
                  pl.BlockSpec((vt.shape[0], seq), lambda b, i: (0, b))],
        out_specs=q_spec,
        out_shape=jax.ShapeDtypeStruct(q.shape, jnp.bfloat16),
        scratch_shapes=[] if bounded else [pltpu.VMEM((2, seq, 2 * TQ), jnp.float32)],
        compiler_params=_params("parallel", "parallel"),
        name="gqa_attention" + ("" if bounded else "_exact"),
    )(bound, q, k2, vt)


def _dilated_attn_kernel(bound_ref, q_ref, k_ref, v_ref, q_cm_ref, k_cm_ref, v_cm_ref, far_ref, near_ref, o_ref,
                         acc_ref, den_ref, max_ref=None, s_ref=None):
    seq = q_ref.shape[0]
    n_chunks = q_ref.shape[1] // LANES
    tile = A_TILE
    n_tiles = seq // tile
    class_len = seq // A_CLASSES
    head0 = _head0_lanes((tile, LANES))
    bound = bound_ref[0]

    def far_where(u):
        t, c = divmod(u, n_chunks)
        return t, c, slice(t * tile, (t + 1) * tile), slice(c * LANES, (c + 1) * LANES)

    def far_scores(u):
        _, _, rows, lanes = far_where(u)
        return _nt_dot(_stack_heads(q_cm_ref[rows, lanes], head0), k_cm_ref[rows, lanes]) + far_ref[...]

    def far_pv(u, keys, p):
        _, _, rows, lanes = far_where(u)
        return jnp.dot(p, v_cm_ref[rows, lanes][keys], preferred_element_type=jnp.float32)

    def far_emit(u, pv, den, m):
        t, c, _, _ = far_where(u)
        for ref, stat in ((acc_ref, pv), (den_ref, den), (max_ref, m)):
            if stat is None:
                continue
            stat = _unstack_heads(stat, head0)
            for e in range(tile // class_len):
                cls = t * (tile // class_len) + e
                ref[c, pl.ds(cls, class_len, stride=A_CLASSES), :] = stat[e * class_len:(e + 1) * class_len]

    _softmax_units(bound, s_ref, n_tiles * n_chunks, far_scores, far_pv, far_emit, 1)

    def near_where(u):
        i, c = divmod(u, n_chunks)
        return i, c, max(i - 1, 0), min(i + 1, n_tiles - 1), slice(c * LANES, (c + 1) * LANES)

    def near_scores(u):
        i, _, lo, hi, lanes = near_where(u)
        qs = _stack_heads(q_ref[i * tile:(i + 1) * tile, lanes], head0)
        return jnp.concatenate([_nt_dot(qs, k_ref[j * tile:(j + 1) * tile, lanes]) + near_ref[j - i + 1]
                                for j in range(lo, hi + 1)], axis=1)

    def near_pv(u, keys, p):
        _, _, lo, hi, lanes = near_where(u)
        return jnp.dot(p, v_ref[lo * tile:(hi + 1) * tile, lanes][keys], preferred_element_type=jnp.float32)

    def near_emit(u, pv, den, m):
        i, c, _, _, lanes = near_where(u)
        rows = slice(i * tile, (i + 1) * tile)
        acc, den = _unstack_heads(pv, head0), _unstack_heads(den, head0)
        far_acc, far_den = acc_ref[c, rows, :], den_ref[c, rows, :]
        if m is not None:
            m, far_m = _unstack_heads(m, head0), max_ref[c, rows, :]
            top = jnp.maximum(m, far_m)
            w_near, w_far = jnp.exp2(m - top), jnp.exp2(far_m - top)
            acc, den, far_acc, far_den = acc * w_near, den * w_near, far_acc * w_far, far_den * w_far
        o_ref[rows, lanes] = ((acc + far_acc) / (den + far_den)).astype(o_ref.dtype)

    _softmax_units(bound, s_ref, n_tiles * n_chunks, near_scores, near_pv, near_emit, 1)


def _dilated_attention(bound, q, k, v, q_cm, k_cm, v_cm, far_bias, near_bias, seq, bounded):
    tokens, width = q.shape
    seq_spec = pl.BlockSpec((seq, width), lambda b: (b, 0))
    stat = pltpu.VMEM((width // LANES, seq, LANES), jnp.float32)
    exact_scratch = [stat, pltpu.VMEM((2, 2 * A_TILE, 3 * A_TILE), jnp.float32)]
    return pl.pallas_call(
        _dilated_attn_kernel,
        grid=(tokens // seq,),
        in_specs=[_SMEM_SPEC] + [seq_spec] * 6 + [_const_spec(far_bias.shape), _const_spec(near_bias.shape)],
        out_specs=seq_spec,
        out_shape=jax.ShapeDtypeStruct(q.shape, jnp.bfloat16),
        scratch_shapes=[stat, stat] + ([] if bounded else exact_scratch),
        compiler_params=_params("parallel"),
        name="dilated_attention" + ("" if bounded else "_exact"),
    )(bound, q, k, v, q_cm, k_cm, v_cm, far_bias, near_bias)


def _c_geometry(rows_total):
    n_tiles = rows_total // C_TILE_ROWS
    wr = min(C_WIN_ROWS, rows_total)
    key_units = C_KEY_ROWS // C_TILE_ROWS
    return n_tiles, wr, key_units


def _nbr_bias_kernel(rpb_ref, o_ref, *, rows_total, layer):
    n_tiles, wr, key_units = _c_geometry(rows_total)
    h = pl.program_id(0) + layer * pl.num_programs(0)
    shape = (GRID_W, LANES)
    c = lax.broadcasted_iota(jnp.int32, shape, 0)
    lane = lax.broadcasted_iota(jnp.int32, shape, 1)
    kc = lane % GRID_W
    d = kc - c
    start = jnp.clip(c - C_WIN_COLS // 2, 0, GRID_W - C_WIN_COLS)
    in_window = (kc >= start) & (kc < start + C_WIN_COLS)
    neg = jnp.full(shape, NEG_INF, jnp.float32)
    n_dr = 2 * C_WIN_ROWS - 1
    n_dc = 2 * C_WIN_COLS - 1
    col_bias = []
    for dri in range(n_dr):
        t = neg
        for j in range(n_dc):
            t = jnp.where(d == j - (C_WIN_COLS - 1), rpb_ref[h * n_dr + dri, j], t)
        col_bias.append(jnp.where(in_window, t * LOG2_E, neg))
    left = lane < GRID_W
    for case, tile in enumerate((0, 1, n_tiles - 1)):
        key_row0 = C_TILE_ROWS * min(max(tile - 1, 0), n_tiles - key_units)
        for rq in range(C_TILE_ROWS):
            r = tile * C_TILE_ROWS + rq
            first = min(max(r - wr // 2, 0), rows_total - wr)
            for pair in range(C_KEY_ROWS // 2):
                halves = []
                for e in range(2):
                    kr = key_row0 + 2 * pair + e
                    halves.append(col_bias[kr - r + C_WIN_ROWS - 1] if first <= kr < first + wr else neg)
                o_ref[case, 0, rq * GRID_W:(rq + 1) * GRID_W, pair * LANES:(pair + 1) * LANES] = (
                    jnp.where(left, halves[0], halves[1]))


def _nbr_bias_table(rpb, rows_total, layer):
    depth, heads, n_dr, n_dc = rpb.shape
    tq = C_TILE_ROWS * GRID_W
    tk = C_KEY_ROWS * GRID_W
    return pl.pallas_call(
        functools.partial(_nbr_bias_kernel, rows_total=rows_total, layer=layer),
        grid=(heads,),
        in_specs=[pl.BlockSpec(memory_space=pltpu.SMEM)],
        out_specs=pl.BlockSpec((3, 1, tq, tk), lambda h: (0, h, 0, 0)),
        out_shape=jax.ShapeDtypeStruct((3, heads, tq, tk), jnp.float32),
        compiler_params=_params("parallel"),
        name="nbr_bias_table",
    )(rpb.reshape(depth * heads * n_dr, n_dc))


def _nbr_attn_kernel(bound_ref, q_ref, k_ref, v_ref, bias_ref, o_ref, s_ref=None, *, rows_total):
    n_tiles, _, key_units = _c_geometry(rows_total)
    tq = C_TILE_ROWS * GRID_W
    n_heads = 2 * (q_ref.shape[1] // LANES)
    head0 = _head0_lanes((tq, LANES))

    def where(u):
        tile, head = divmod(u, n_heads)
        key0 = tq * min(max(tile - 1, 0), n_tiles - key_units)
        case = 0 if tile == 0 else (2 if tile == n_tiles - 1 else 1)
        lanes = slice((head // 2) * LANES, (head // 2 + 1) * LANES)
        return tile, head, slice(key0, key0 + key_units * tq), case, lanes

    def scores(u):
        tile, head, keys, case, lanes = where(u)
        q = q_ref[tile * tq:(tile + 1) * tq, lanes]
        qm = jnp.where(head0 if head % 2 == 0 else ~head0, q, jnp.zeros_like(q))
        return _nt_dot(qm, k_ref[keys, lanes]) + bias_ref[case, head]

    def pv(u, piece, p):
        _, _, keys, _, lanes = where(u)
        return jnp.dot(p, v_ref[keys, lanes][piece], preferred_element_type=jnp.float32)

    pair = []

    def emit(u, acc, den, m):
        tile, head, _, _, lanes = where(u)
        pair.append(acc / den)
        if head % 2 == 1:
            o_ref[tile * tq:(tile + 1) * tq, lanes] = jnp.where(head0, *pair).astype(o_ref.dtype)
            pair.clear()

    _softmax_units(bound_ref[0], s_ref, n_tiles * n_heads, scores, pv, emit, 1)


def _nbr_attention(bound, q, k, v, bias, seq, bounded):
    tokens, width = q.shape
    rows_total = seq // GRID_W
    _, _, key_units = _c_geometry(rows_total)
    tq = C_TILE_ROWS * GRID_W
    seq_spec = pl.BlockSpec((seq, width), lambda b: (b, 0))
    return pl.pallas_call(
        functools.partial(_nbr_attn_kernel, rows_total=rows_total),
        grid=(tokens // seq,),
        in_specs=[_SMEM_SPEC, seq_spec, seq_spec, seq_spec, _const_spec(bias.shape)],
        out_specs=seq_spec,
        out_shape=jax.ShapeDtypeStruct(q.shape, jnp.bfloat16),
        scratch_shapes=[] if bounded else [pltpu.VMEM((2, tq, key_units * tq), jnp.float32)],
        compiler_params=_params("parallel"),
        name="nbr_attention" + ("" if bounded else "_exact"),
    )(bound, q, k, v, bias)


def _merge_kernel(x_ref, g_ref, oa_ref, ob_ref, oc_ref, wg_ref, bg_ref, wbr_ref, wo_ref, o_ref):
    x = x_ref[...]
    h = _rms_normed_bf16(x, g_ref[...])
    merged = None
    row = 0
    for n, br_ref in enumerate((oa_ref, ob_ref, oc_ref)):
        cols = slice(n * D_MODEL, (n + 1) * D_MODEL)
        logits = jnp.dot(h, wg_ref[:, cols], preferred_element_type=jnp.float32) + bg_ref[:, cols]
        gate = 1.0 / (1.0 + jnp.exp(-logits))
        width = br_ref.shape[1]
        y = jnp.dot(br_ref[...], wbr_ref[row:row + width, :], preferred_element_type=jnp.float32)
        row += width
        merged = gate * y if merged is None else merged + gate * y
    o_ref[...] = x + jnp.dot(merged.astype(jnp.bfloat16), wo_ref[...], preferred_element_type=jnp.float32)


def _merge(x2, norm_g, oa, ob, oc, w_gate, b_gate, w_branch, w_out, layer):
    tokens = x2.shape[0]
    tm = TM_MERGE
    row = lambda width: pl.BlockSpec((tm, width), lambda i: (i, 0))
    return pl.pallas_call(
        _merge_kernel,
        grid=(tokens // tm,),
        in_specs=[row(D_MODEL), _layer_spec(norm_g.shape, layer), row(oa.shape[1]), row(ob.shape[1]),
                  row(oc.shape[1]), _layer_spec(w_gate.shape, layer), _layer_spec(b_gate.shape, layer),
                  _layer_spec(w_branch.shape, layer), _layer_spec(w_out.shape, layer)],
        out_specs=row(D_MODEL),
        out_shape=jax.ShapeDtypeStruct(x2.shape, jnp.float32),
        compiler_params=_params("parallel"),
        name="gated_merge",
    )(x2, norm_g, oa, ob, oc, w_gate, b_gate, w_branch, w_out)


def _gelu_tanh(x):
    return 0.5 * x * (1.0 + jnp.tanh(np.sqrt(2.0 / np.pi).astype(np.float32) * (x + 0.044715 * (x * x * x))))


def _ffn_kernel(x_ref, prev_ref, next_ref, g_ref, w_ref, cw_ref, cb_ref, wd_ref, o_ref, h_ref, u_ref,
                *, tiles_per_seq):
    i = pl.program_id(0)
    tm = x_ref.shape[0]
    g = g_ref[...]
    has_prev = (i % tiles_per_seq != 0).astype(jnp.float32)
    has_next = (i % tiles_per_seq != tiles_per_seq - 1).astype(jnp.float32)
    h_ref[0:HALO, :] = _rms_normed_bf16(prev_ref[...] * has_prev, g)
    h_ref[HALO:HALO + tm, :] = _rms_normed_bf16(x_ref[...], g)
    h_ref[HALO + tm:, :] = _rms_normed_bf16(next_ref[...] * has_next, g)
    h = h_ref[...]

    def conv(start, u_scr):
        u = jnp.dot(h, w_ref[:, start:start + FC_SUB], preferred_element_type=jnp.float32)
        outs = []
        for ct in range(FC_SUB // LANES):
            u_scr[ct] = u[:, ct * LANES:(ct + 1) * LANES]
            cols = slice(start + ct * LANES, start + (ct + 1) * LANES)
            out = cb_ref[:, cols] + u_scr[ct, pl.ds(HALO - 1, tm), :] * cw_ref[0:1, cols]
            for j in range(1, CONV_WIDTH):
                out = out + u_scr[ct, pl.ds(HALO - 1 + j, tm), :] * cw_ref[j:j + 1, cols]
            outs.append(out)
        return jnp.concatenate(outs, axis=1)

    acc = x_ref[...]
    for group in range(D_FF // FC_DOWN):
        acts = []
        for s in range(FC_DOWN // FC_SUB):
            n = group * (FC_DOWN // FC_SUB) + s
            slot = u_ref.at[n % FFN_U_SLOTS]
            act = _gelu_tanh(conv(n * FC_SUB, slot.at[0])) * conv(D_FF + n * FC_SUB, slot.at[1])
            acts.append(act.astype(jnp.bfloat16))
        rows = slice(group * FC_DOWN, (group + 1) * FC_DOWN)
        acc = acc + jnp.dot(jnp.concatenate(acts, axis=1), wd_ref[rows, :], preferred_element_type=jnp.float32)
    o_ref[...] = acc


def _ffn(x2, norm_g, w_up, conv_w, conv_b, w_down, seq, layer):
    tokens = x2.shape[0]
    tm = TM_FFN
    halo_blocks = tm // HALO
    last_halo = tokens // HALO - 1
    in_specs = [
        pl.BlockSpec((tm, D_MODEL), lambda i: (i, 0)),
        pl.BlockSpec((HALO, D_MODEL), lambda i: (jnp.maximum(i * halo_blocks - 1, 0), 0)),
        pl.BlockSpec((HALO, D_MODEL), lambda i: (jnp.minimum((i + 1) * halo_blocks, last_halo), 0)),
        _layer_spec(norm_g.shape, layer), _layer_spec(w_up.shape, layer), _layer_spec(conv_w.shape, layer),
        _layer_spec(conv_b.shape, layer), _layer_spec(w_down.shape, layer),
    ]
    return pl.pallas_call(
        functools.partial(_ffn_kernel, tiles_per_seq=seq // tm),
        grid=(tokens // tm,),
        in_specs=in_specs,
        out_specs=pl.BlockSpec((tm, D_MODEL), lambda i: (i, 0)),
        out_shape=jax.ShapeDtypeStruct(x2.shape, jnp.float32),
        scratch_shapes=[pltpu.VMEM((tm + 2 * HALO, D_MODEL), jnp.bfloat16),
                        pltpu.VMEM((FFN_U_SLOTS, 2, FC_SUB // LANES, tm + 2 * HALO, LANES), jnp.float32)],
        compiler_params=_params("parallel"),
        name="ffn",
    )(x2, x2, x2, norm_g, w_up, conv_w, conv_b, w_down)


def _score_bounds(qk_gain, rel_pos_bias):
    g = jnp.max(jnp.abs(qk_gain), axis=-1)
    qk = 1.05 * HEAD_DIM * (HEAD_DIM ** -0.5 * LOG2_E) * g[..., 0] * g[..., 1]
    rpb = rel_pos_bias.reshape(rel_pos_bias.shape[0], -1) * LOG2_E
    zero = jnp.zeros_like(qk[:, 0])
    bias_hi = jnp.stack([zero + float(np.max(_dilated_near_bias())), zero, rpb.max(axis=1)], axis=1)
    bias_lo = jnp.stack([zero, zero, rpb.min(axis=1)], axis=1)
    return qk + bias_hi, 2.0 * qk + bias_hi - bias_lo <= SCORE_SPREAD_LIMIT


def _mixers(bounds, qa, ka, va, qa_cm, ka_cm, va_cm, far_bias, near_bias, qb, kb2, vbt, qc, kc, vc, nbr_bias,
            *, seq, bounded):
    oa = _dilated_attention(bounds[0:1], qa, ka, va, qa_cm, ka_cm, va_cm, far_bias, near_bias, seq, bounded)
    ob = _gqa_attention(bounds[1:2], qb, kb2, vbt, seq, bounded)
    oc = _nbr_attention(bounds[2:3], qc, kc, vc, nbr_bias, seq, bounded)
    return oa, ob, oc


def kernel(x, w_in, b_gate, qk_gain, rel_pos_bias, w_branch, w_out, norm_mix, norm_ffn, w_up, conv_w,
           conv_b, w_down):
    batch, seq, d_model = x.shape
    depth = w_in.shape[0]
    assert d_model == D_MODEL and seq % TM_FFN == 0 and seq % A_TILE == 0 and seq % (A_CLASSES * 16) == 0
    bf16 = jnp.bfloat16
    x2 = x.reshape(batch * seq, d_model)

    w_qkv = w_in[:, :, :QKV_WIDTH].astype(bf16)
    w_gate = w_in[:, :, QKV_WIDTH:].astype(bf16)
    gains = jnp.tile(qk_gain.reshape(depth, 2 * N_BRANCH, HEAD_DIM), (1, 1, LANES // HEAD_DIM))
    w_br, w_out_bf, w_up_bf, w_down_bf = (w.astype(bf16) for w in (w_branch, w_out, w_up, w_down))
    norm_mix3, norm_ffn3 = norm_mix[:, None, :], norm_ffn[:, None, :]
    b_gate3, conv_b3 = b_gate[:, None, :], conv_b[:, None, :]
    mavg = jnp.asarray(_head_mean_matrix(), bf16)
    ropes = _rope_tables(seq)
    far_bias = jnp.asarray(_dilated_far_bias(seq))
    near_bias = jnp.asarray(_dilated_near_bias())
    bounds, within_limit = _score_bounds(qk_gain, rel_pos_bias)

    for l in range(depth):
        (qa, ka, va, qb, kb2, vbt, qc, kc, vc), (qa_cm, ka_cm, va_cm) = _in_projection(
            x2, norm_mix3, w_qkv, gains, mavg, ropes, seq, l)
        nbr_bias = _nbr_bias_table(rel_pos_bias, seq // GRID_W, l)
        oa, ob, oc = lax.cond(jnp.all(within_limit[l]), functools.partial(_mixers, seq=seq, bounded=True),
                              functools.partial(_mixers, seq=seq, bounded=False), bounds[l],
                              qa, ka, va, qa_cm, ka_cm, va_cm, far_bias, near_bias, qb, kb2, vbt, qc, kc, vc, nbr_bias)
        x2 = _merge(x2, norm_mix3, oa, ob, oc, w_gate, b_gate3, w_br, w_out_bf, l)
        x2 = _ffn(x2, norm_ffn3, w_up_bf, conv_w, conv_b3, w_down_bf, seq, l)
    return x2.reshape(batch, seq, d_model)
```
